```python
import math
import jax
import jax.numpy as jnp
from jax import lax
import numpy as np

D_MODEL = 2048
BATCH = 4
SEQ = 2048
DEPTH = 4
DEC_BATCH = 128
DEC_SEQ = 1
PAST_LEN = 8192
PAGE_SIZE = 128

N_META = 16
EPS = 1e-6
NEG_INF = -1e30
Q_BLOCK = 128

A_HEADS = 8
A_DK = 128
A_DV = 128
A_CONV = 4
A_CHUNK = 64
A_QK = A_HEADS * A_DK
A_V = A_HEADS * A_DV
A_CONV_CH = 2 * A_QK + A_V

B_HEADS = 8
B_KV_HEADS = 2
B_HD = 128
B_GROUP = B_HEADS // B_KV_HEADS
FOX_SCALE = B_HD ** -0.5

C_HEADS = 16
C_Q_RANK = 512
C_KV_RANK = 512
C_NOPE = 128
C_ROPE = 64
C_VD = 128
C_IN = C_Q_RANK + C_KV_RANK + C_ROPE
ROPE_THETA = 10000.0
MLA_SCALE = (C_NOPE + C_ROPE) ** -0.5

N_GROUPS = 8
EXPERTS_PER_GROUP = 8
N_EXPERTS = N_GROUPS * EXPERTS_PER_GROUP
TOP_K = 2
D_EXPERT = 384
MOE_BLOCK = 128

N_AB_LAYERS = (DEPTH + 1) // 2
N_C_LAYERS = DEPTH // 2
AB_SIZES = (A_CONV_CH, A_V, A_HEADS, A_HEADS, B_HEADS * B_HD, B_KV_HEADS * B_HD, B_KV_HEADS * B_HD, B_HEADS)
AB_IN = sum(AB_SIZES)
AB_SPLITS = tuple(int(s) for s in np.cumsum(AB_SIZES)[:-1])
AB_MIX = A_V + B_HEADS * B_HD

kernel_name = 'hybrid_gdn_fox_mla_hmoe_step'


def rms_norm(x, g):
    xf = x.astype(jnp.float32)
    y = xf * lax.rsqrt(jnp.mean(xf * xf, axis=-1, keepdims=True) + EPS)
    return (y * g.astype(jnp.float32)).astype(x.dtype)


def l2_norm(x):
    xf = x.astype(jnp.float32)
    return (xf * lax.rsqrt(jnp.sum(xf * xf, axis=-1, keepdims=True) + EPS)).astype(x.dtype)


def rope(x, pos):
    half = C_ROPE // 2
    inv = ROPE_THETA ** (-jnp.arange(half, dtype=jnp.float32) / half)
    ang = pos.astype(jnp.float32)[:, None] * inv[None, :]
    cos = jnp.cos(ang)[None, :, None, :]
    sin = jnp.sin(ang)[None, :, None, :]
    xf = x.astype(jnp.float32)
    x1, x2 = xf[..., :half], xf[..., half:]
    return jnp.concatenate([x1 * cos - x2 * sin, x2 * cos + x1 * sin], axis=-1).astype(x.dtype)


def to_blocks(a, pad):
    a = jnp.pad(a, [(0, 0), (pad, 0)] + [(0, 0)] * (a.ndim - 2))
    nblk = a.shape[1] // Q_BLOCK
    return jnp.moveaxis(a.reshape(a.shape[0], nblk, Q_BLOCK, *a.shape[2:]), 1, 0)


def from_blocks(o, pad):
    o = jnp.moveaxis(o, 0, 1)
    o = o.reshape(o.shape[0], -1, *o.shape[3:])
    return o[:, pad:]


def ab_pre(h, conv_buf, w_in, conv_w, a_log, dt_bias, fgate_bias, q_norm, k_norm):
    b, t, _ = h.shape
    qkv_a, z, a_raw, b_raw, q_b, k_b, v_b, f_raw = jnp.split(h @ w_in, AB_SPLITS, axis=-1)
    xp = jnp.concatenate([conv_buf.astype(qkv_a.dtype), qkv_a], axis=1)
    conv = xp[:, 0:t] * conv_w[0]
    for i in range(1, A_CONV):
        conv = conv + xp[:, i:i + t] * conv_w[i]
    conv = jax.nn.silu(conv)
    new_buf = xp[:, t:]
    qa, ka, va = jnp.split(conv, [A_QK, 2 * A_QK], axis=-1)
    qa = l2_norm(qa.reshape(b, t, A_HEADS, A_DK)) * (A_DK ** -0.5)
    ka = l2_norm(ka.reshape(b, t, A_HEADS, A_DK))
    va = va.reshape(b, t, A_HEADS, A_DV)
    beta = jax.nn.sigmoid(b_raw.astype(jnp.float32))
    g = -jnp.exp(a_log.astype(jnp.float32)) * jax.nn.softplus(a_raw.astype(jnp.float32) + dt_bias.astype(jnp.float32))
    qb = rms_norm(q_b.reshape(b, t, B_HEADS, B_HD), q_norm)
    kb = rms_norm(k_b.reshape(b, t, B_KV_HEADS, B_HD), k_norm)
    vb = v_b.reshape(b, t, B_KV_HEADS, B_HD)
    logf = jax.nn.log_sigmoid(f_raw.astype(jnp.float32) + fgate_bias.astype(jnp.float32))
    return (qa, ka, va, g, beta, z, new_buf), (qb, kb, vb, logf)


def gdn_chunked(q, k, v, g, beta):
    b, t, _, _ = q.shape
    n_chunks = -(-t // A_CHUNK)
    pad = n_chunks * A_CHUNK - t

    def chunk(a):
        a = jnp.pad(a, [(0, 0), (pad, 0)] + [(0, 0)] * (a.ndim - 2))
        a = a.reshape(b, n_chunks, A_CHUNK, *a.shape[2:])
        return jnp.moveaxis(a, 3, 1)

    qc, kc, vc = (chunk(a.astype(jnp.float32)) for a in (q, k, v))
    gc, bc = chunk(g), chunk(beta)
    G = jnp.cumsum(gc, axis=-1)
    idx = jnp.arange(A_CHUNK)
    incl = idx[:, None] >= idx[None, :]
    strict = idx[:, None] > idx[None, :]
    decay = jnp.where(incl, jnp.exp(jnp.where(incl, G[..., :, None] - G[..., None, :], 0.0)), 0.0)
    kbeta = kc * bc[..., None]
    A = jnp.where(strict, jnp.einsum('bhnik,bhnjk->bhnij', kbeta, kc) * decay, 0.0)
    L = A + jnp.eye(A_CHUNK, dtype=jnp.float32)
    w = lax.linalg.triangular_solve(L, kbeta * jnp.exp(G)[..., None], left_side=True, lower=True, unit_diagonal=True)
    u = lax.linalg.triangular_solve(L, vc * bc[..., None], left_side=True, lower=True, unit_diagonal=True)
    att = jnp.einsum('bhnik,bhnjk->bhnij', qc, kc) * decay
    qg = qc * jnp.exp(G)[..., None]
    kg = kc * jnp.exp(G[..., -1:] - G)[..., None]
    gl = jnp.exp(G[..., -1])

    def step(S, inp):
        qg_, kg_, w_, u_, att_, gl_ = inp
        v_new = u_ - jnp.einsum('bhck,bhkv->bhcv', w_, S)
        o = jnp.einsum('bhck,bhkv->bhcv', qg_, S) + jnp.einsum('bhij,bhjv->bhiv', att_, v_new)
        S = S * gl_[..., None, None] + jnp.einsum('bhck,bhcv->bhkv', kg_, v_new)
        return S, o

    xs = tuple(jnp.moveaxis(a, 2, 0) for a in (qg, kg, w, u, att, gl))
    S0 = jnp.zeros((b, A_HEADS, A_DK, A_DV), jnp.float32)
    S, o = lax.scan(step, S0, xs)
    o = jnp.moveaxis(o, 0, 2).reshape(b, A_HEADS, -1, A_DV)[:, :, pad:]
    return jnp.moveaxis(o, 1, 2), S


def gdn_recurrent(q, k, v, g, beta, S0):
    def step(S, inp):
        qt, kt, vt, gt, bt = inp
        S = S * jnp.exp(gt)[..., None, None]
        pred = jnp.einsum('bhk,bhkv->bhv', kt, S)
        S = S + jnp.einsum('bhk,bhv->bhkv', kt, (vt - pred) * bt[..., None])
        return S, jnp.einsum('bhk,bhkv->bhv', qt, S)

    xs = tuple(jnp.moveaxis(a.astype(jnp.float32), 1, 0) for a in (q, k, v, g, beta))
    S, o = lax.scan(step, S0.astype(jnp.float32), xs)
    return jnp.moveaxis(o, 0, 1), S


def fox_prompt(q, k, v, logf):
    b, t = q.shape[:2]
    F = jnp.cumsum(logf, axis=1).reshape(b, t, B_KV_HEADS, B_GROUP)
    nblk = -(-t // Q_BLOCK)
    pad = nblk * Q_BLOCK - t
    qb = to_blocks(q.reshape(b, t, B_KV_HEADS, B_GROUP, B_HD), pad)
    Fb = to_blocks(F, pad)
    pb = (jnp.arange(nblk * Q_BLOCK) - pad).reshape(nblk, Q_BLOCK)
    Fk = jnp.moveaxis(F, 1, -1)
    kpos = jnp.arange(t)

    def block(args):
        qblk, Fq, qpos = args
        s = jnp.einsum('bqkgd,bskd->bkgqs', qblk, k).astype(jnp.float32) * FOX_SCALE
        s = s + jnp.moveaxis(Fq, 1, -1)[..., None] - Fk[..., None, :]
        s = jnp.where(kpos[None, :] <= qpos[:, None], s, NEG_INF)
        p = jax.nn.softmax(s, axis=-1).astype(v.dtype)
        return jnp.einsum('bkgqs,bskd->bqkgd', p, v)

    o = lax.map(block, (qb, Fb, pb))
    return from_blocks(o, pad).reshape(b, t, B_HEADS * B_HD)


def fox_sample(q, k, v, logf, cache_k, cache_v, cache_logf, page_table, li):
    ds = q.shape[1]
    past = page_table.shape[1] * cache_k.shape[2]
    s_pos = jnp.arange(past + ds)
    mask = s_pos[None, :] <= (past + jnp.arange(ds))[:, None]

    def one(args):
        qs, ks_, vs, ls, pt = args
        kk = jnp.concatenate([cache_k[li, pt].reshape(past, B_KV_HEADS, B_HD).astype(ks_.dtype), ks_], axis=0)
        vv = jnp.concatenate([cache_v[li, pt].reshape(past, B_KV_HEADS, B_HD).astype(vs.dtype), vs], axis=0)
        ll = jnp.concatenate([cache_logf[li, pt].reshape(past, B_HEADS).astype(jnp.float32), ls], axis=0)
        Fs = jnp.moveaxis(jnp.cumsum(ll, axis=0).reshape(past + ds, B_KV_HEADS, B_GROUP), 0, -1)
        Fq = Fs[..., past:]
        s = jnp.einsum('qkgd,skd->kgqs', qs.reshape(ds, B_KV_HEADS, B_GROUP, B_HD), kk).astype(jnp.float32) * FOX_SCALE
        s = jnp.where(mask, s + Fq[..., None] - Fs[..., None, :], NEG_INF)
        p = jax.nn.softmax(s, axis=-1).astype(vv.dtype)
        return jnp.einsum('kgqs,skd->qkgd', p, vv).reshape(ds, B_HEADS * B_HD)

    return lax.map(one, (q, k, v, logf, page_table))


def ab_post(o_gdn, z, o_fox, gdn_norm, w_out):
    b, t = o_gdn.shape[:2]
    og = rms_norm(o_gdn, gdn_norm) * jax.nn.silu(z.reshape(b, t, A_HEADS, A_DV).astype(jnp.float32))
    mixed = jnp.concatenate([og.reshape(b, t, A_V).astype(o_fox.dtype), o_fox], axis=-1)
    return mixed @ w_out


def mla_pre(h, pos, w_in, q_a_norm, kv_a_norm, w_q_b, q_nope_norm, q_rope_norm, k_rope_norm):
    q_lat, c, kr = jnp.split(h @ w_in, [C_Q_RANK, C_Q_RANK + C_KV_RANK], axis=-1)
    q = jnp.einsum('btr,rhd->bthd', rms_norm(q_lat, q_a_norm), w_q_b)
    q_nope = rms_norm(q[..., :C_NOPE], q_nope_norm)
    q_rope = rope(rms_norm(q[..., C_NOPE:], q_rope_norm), pos)
    c = rms_norm(c, kv_a_norm)
    kr = rope(rms_norm(kr, k_rope_norm)[:, :, None, :], pos)[:, :, 0]
    return q_nope, q_rope, c, kr


def mla_prompt(q_nope, q_rope, c, kr, w_uk, w_uv, k_nope_norm):
    b, t = c.shape[:2]
    k_nope = rms_norm(jnp.einsum('btr,rhd->bthd', c, w_uk), k_nope_norm)
    v = jnp.einsum('btr,rhd->bthd', c, w_uv)
    nblk = -(-t // Q_BLOCK)
    pad = nblk * Q_BLOCK - t
    qnb = to_blocks(q_nope, pad)
    qrb = to_blocks(q_rope, pad)
    pb = (jnp.arange(nblk * Q_BLOCK) - pad).reshape(nblk, Q_BLOCK)
    kpos = jnp.arange(t)

    def block(args):
        qn, qr, qpos = args
        s = (jnp.einsum('bqhd,bshd->bhqs', qn, k_nope) + jnp.einsum('bqhd,bsd->bhqs', qr, kr)).astype(jnp.float32) * MLA_SCALE
        s = jnp.where(kpos[None, :] <= qpos[:, None], s, NEG_INF)
        p = jax.nn.softmax(s, axis=-1).astype(v.dtype)
        return jnp.einsum('bhqs,bshd->bqhd', p, v)

    o = lax.map(block, (qnb, qrb, pb))
    return from_blocks(o, pad).reshape(b, t, C_HEADS * C_VD)


def mla_sample(q_nope, q_rope, c_new, kr_new, cache_lat, cache_rope, page_table, li, w_uk, w_uv, k_nope_norm):
    ds = q_nope.shape[1]
    past = page_table.shape[1] * cache_lat.shape[2]
    mask = jnp.arange(past + ds)[None, :] <= (past + jnp.arange(ds))[:, None]

    def one(args):
        qn, qr, cn, krn, pt = args
        cc = jnp.concatenate([cache_lat[li, pt].reshape(past, C_KV_RANK).astype(cn.dtype), cn], axis=0)
        rr = jnp.concatenate([cache_rope[li, pt].reshape(past, C_ROPE).astype(krn.dtype), krn], axis=0)
        kn = rms_norm(jnp.einsum('sr,rhd->shd', cc, w_uk), k_nope_norm)
        s = (jnp.einsum('qhd,shd->hqs', qn, kn) + jnp.einsum('qhd,sd->hqs', qr, rr)).astype(jnp.float32) * MLA_SCALE
        s = jnp.where(mask, s, NEG_INF)
        p = jax.nn.softmax(s, axis=-1).astype(cc.dtype)
        o_lat = jnp.einsum('hqs,sr->qhr', p, cc)
        return jnp.einsum('qhr,rhd->qhd', o_lat, w_uv).reshape(ds, C_HEADS * C_VD)

    return lax.map(one, (q_nope, q_rope, c_new, kr_new, page_table))


def hier_moe(x, w_group, b_group, w_expert, b_expert, w_gate, w_up, w_down):
    n = x.shape[0]
    xf = x.astype(jnp.float32)
    grp_prob = jax.nn.softmax(xf @ w_group.astype(jnp.float32) + b_group.astype(jnp.float32), axis=-1)
    grp_p, grp_i = lax.top_k(grp_prob, 1)
    e_logits = (xf @ w_expert.astype(jnp.float32) + b_expert.astype(jnp.float32)).reshape(n, N_GROUPS, EXPERTS_PER_GROUP)
    e_in = jnp.take_along_axis(e_logits, grp_i[:, :, None], axis=1)[:, 0]
    top_p, top_i = lax.top_k(jax.nn.softmax(e_in, axis=-1), TOP_K)
    gate = grp_p * top_p / jnp.sum(top_p, axis=-1, keepdims=True)
    expert = grp_i * EXPERTS_PER_GROUP + top_i
    nk = n * TOP_K
    flat_e = expert.reshape(-1)
    flat_tok = jnp.repeat(jnp.arange(n, dtype=jnp.int32), TOP_K)
    order = jnp.argsort(flat_e)
    se, stok, sgate = flat_e[order], flat_tok[order], gate.reshape(-1)[order]
    counts = jnp.zeros((N_EXPERTS,), jnp.int32).at[flat_e].add(1)
    padded = (counts + MOE_BLOCK - 1) // MOE_BLOCK * MOE_BLOCK
    pad_end = jnp.cumsum(padded)
    pad_start = pad_end - padded
    seg_start = jnp.cumsum(counts) - counts
    dest = pad_start[se] + jnp.arange(nk, dtype=jnp.int32) - seg_start[se]
    n_slots = (nk + N_EXPERTS * (MOE_BLOCK - 1) + MOE_BLOCK - 1) // MOE_BLOCK * MOE_BLOCK
    n_blocks = n_slots // MOE_BLOCK
    slot_tok = jnp.full((n_slots,), n, jnp.int32).at[dest].set(stok)
    slot_gate = jnp.zeros((n_slots,), jnp.float32).at[dest].set(sgate)
    block_expert = jnp.minimum(jnp.searchsorted(pad_end, jnp.arange(n_blocks, dtype=jnp.int32) * MOE_BLOCK, side='right'), N_EXPERTS - 1)
    x_pad = jnp.concatenate([x, jnp.zeros((1, x.shape[1]), x.dtype)], axis=0)
    xs = x_pad[slot_tok].reshape(n_blocks, MOE_BLOCK, x.shape[1])

    def expert_block(args):
        xb, e = args
        hb = jax.nn.silu(xb @ w_gate[e]) * (xb @ w_up[e])
        return hb @ w_down[e]

    ys = lax.map(expert_block, (xs, block_expert)).reshape(n_slots, x.shape[1])
    out = jax.ops.segment_sum(ys.astype(jnp.float32) * slot_gate[:, None], slot_tok, num_segments=n + 1)[:n]
    return out.astype(x.dtype)


def setup_inputs(seed: int = 0) -> dict:
    key = jax.random.key(seed)
    ks = iter(jax.random.split(key, 64))
    f32 = jnp.float32

    def nrm(shape, scale=1.0):
        return jax.random.normal(next(ks), shape, f32) * scale

    def gain(shape):
        return 1.0 + 0.1 * jax.random.normal(next(ks), shape, f32)

    n_pages = PAST_LEN // PAGE_SIZE
    n_used = DEC_BATCH * n_pages
    n_pool = n_used + max(1, n_used // 4)
    page_table = jax.random.permutation(next(ks), n_pool)[:n_used].reshape(DEC_BATCH, n_pages).astype(jnp.int32)
    dt = jnp.exp(jax.random.uniform(next(ks), (N_AB_LAYERS, A_HEADS), f32, math.log(1e-3), math.log(1e-1)))
    return {
        'x_prompt': nrm((BATCH, SEQ, D_MODEL)),
        'x_sample': nrm((DEC_BATCH, DEC_SEQ, D_MODEL)),
        'state_gdn': nrm((N_AB_LAYERS, DEC_BATCH, A_HEADS, A_DK, A_DV), A_DK ** -0.5),
        'state_gdn_conv': nrm((N_AB_LAYERS, DEC_BATCH, A_CONV - 1, A_CONV_CH)),
        'cache_fox_k': nrm((N_AB_LAYERS, n_pool, PAGE_SIZE, B_KV_HEADS, B_HD)),
        'cache_fox_v': nrm((N_AB_LAYERS, n_pool, PAGE_SIZE, B_KV_HEADS, B_HD)),
        'cache_fox_logf': jax.nn.log_sigmoid(2.0 + nrm((N_AB_LAYERS, n_pool, PAGE_SIZE, B_HEADS), 0.5)),
        'cache_mla_latent': nrm((N_C_LAYERS, n_pool, PAGE_SIZE, C_KV_RANK)),
        'cache_mla_rope': nrm((N_C_LAYERS, n_pool, PAGE_SIZE, C_ROPE)),
        'page_table': page_table,
        'meta_tokens': nrm((N_META, D_MODEL)),
        'norm_mix': gain((DEPTH, D_MODEL)),
        'norm_ffn': gain((DEPTH, D_MODEL)),
        'ab_w_in': nrm((N_AB_LAYERS, D_MODEL, AB_IN), D_MODEL ** -0.5),
        'ab_conv_w': nrm((N_AB_LAYERS, A_CONV, A_CONV_CH), A_CONV ** -0.5),
        'ab_a_log': jnp.log(jax.random.uniform(next(ks), (N_AB_LAYERS, A_HEADS), f32, 1.0, 16.0)),
        'ab_dt_bias': dt + jnp.log(-jnp.expm1(-dt)),
        'ab_gdn_norm': gain((N_AB_LAYERS, A_DV)),
        'ab_fgate_bias': 2.0 + nrm((N_AB_LAYERS, B_HEADS), 0.5),
        'ab_q_norm': gain((N_AB_LAYERS, B_HD)),
        'ab_k_norm': gain((N_AB_LAYERS, B_HD)),
        'ab_w_out': nrm((N_AB_LAYERS, AB_MIX, D_MODEL), AB_MIX ** -0.5),
        'c_w_in': nrm((N_C_LAYERS, D_MODEL, C_IN), D_MODEL ** -0.5),
        'c_q_a_norm': gain((N_C_LAYERS, C_Q_RANK)),
        'c_kv_a_norm': gain((N_C_LAYERS, C_KV_RANK)),
        'c_w_q_b': nrm((N_C_LAYERS, C_Q_RANK, C_HEADS, C_NOPE + C_ROPE), C_Q_RANK ** -0.5),
        'c_w_uk': nrm((N_C_LAYERS, C_KV_RANK, C_HEADS, C_NOPE), C_KV_RANK ** -0.5),
        'c_w_uv': nrm((N_C_LAYERS, C_KV_RANK, C_HEADS, C_VD), C_KV_RANK ** -0.5),
        'c_q_nope_norm': gain((N_C_LAYERS, C_NOPE)),
        'c_q_rope_norm': gain((N_C_LAYERS, C_ROPE)),
        'c_k_nope_norm': gain((N_C_LAYERS, C_NOPE)),
        'c_k_rope_norm': gain((N_C_LAYERS, C_ROPE)),
        'c_w_out': nrm((N_C_LAYERS, C_HEADS * C_VD, D_MODEL), (C_HEADS * C_VD) ** -0.5),
        'moe_w_group': nrm((DEPTH, D_MODEL, N_GROUPS), D_MODEL ** -0.5),
        'moe_b_group': nrm((DEPTH, N_GROUPS), 0.01),
        'moe_w_expert': nrm((DEPTH, D_MODEL, N_EXPERTS), D_MODEL ** -0.5),
        'moe_b_expert': nrm((DEPTH, N_EXPERTS), 0.01),
        'moe_w_gate': nrm((DEPTH, N_EXPERTS, D_MODEL, D_EXPERT), D_MODEL ** -0.5),
        'moe_w_up': nrm((DEPTH, N_EXPERTS, D_MODEL, D_EXPERT), D_MODEL ** -0.5),
        'moe_w_down': nrm((DEPTH, N_EXPERTS, D_EXPERT, D_MODEL), D_EXPERT ** -0.5),
    }


def reference(x_prompt, x_sample, state_gdn, state_gdn_conv, cache_fox_k, cache_fox_v, cache_fox_logf,
              cache_mla_latent, cache_mla_rope, page_table, meta_tokens, norm_mix, norm_ffn,
              ab_w_in, ab_conv_w, ab_a_log, ab_dt_bias, ab_gdn_norm, ab_fgate_bias, ab_q_norm, ab_k_norm, ab_w_out,
              c_w_in, c_q_a_norm, c_kv_a_norm, c_w_q_b, c_w_uk, c_w_uv, c_q_nope_norm, c_q_rope_norm,
              c_k_nope_norm, c_k_rope_norm, c_w_out,
              moe_w_group, moe_b_group, moe_w_expert, moe_b_expert, moe_w_gate, moe_w_up, moe_w_down):
    b = x_prompt.shape[0]
    meta = jnp.broadcast_to(meta_tokens.astype(x_prompt.dtype)[None], (b, N_META, D_MODEL))
    xp = jnp.concatenate([meta, x_prompt], axis=1)
    xs = x_sample
    t = xp.shape[1]
    ds = xs.shape[1]
    pos_p = jnp.arange(t)
    pos_s = PAST_LEN + jnp.arange(ds)
    pl_S, pl_conv, pl_k, pl_v, pl_f, pl_c, pl_r = [], [], [], [], [], [], []
    sl_S, sl_conv, sl_k, sl_v, sl_f, sl_c, sl_r = [], [], [], [], [], [], []

    for layer in range(DEPTH):
        li = layer // 2
        hp = rms_norm(xp, norm_mix[layer])
        hs = rms_norm(xs, norm_mix[layer])
        if layer % 2 == 0:
            wts = (ab_w_in[li], ab_conv_w[li], ab_a_log[li], ab_dt_bias[li], ab_fgate_bias[li], ab_q_norm[li], ab_k_norm[li])
            (qa, ka, va, g, beta, z, buf), (qb, kb, vb, lf) = ab_pre(hp, jnp.zeros((b, A_CONV - 1, A_CONV_CH), hp.dtype), *wts)
            o_a, S_new = gdn_chunked(qa, ka, va, g, beta)
            o_b = fox_prompt(qb, kb, vb, lf)
            yp = ab_post(o_a, z, o_b, ab_gdn_norm[li], ab_w_out[li])
            pl_S.append(S_new.astype(state_gdn.dtype)); pl_conv.append(buf)
            pl_k.append(kb); pl_v.append(vb); pl_f.append(lf)
            (qa, ka, va, g, beta, z, buf), (qb, kb, vb, lf) = ab_pre(hs, state_gdn_conv[li], *wts)
            o_a, S_new = gdn_recurrent(qa, ka, va, g, beta, state_gdn[li])
            o_b = fox_sample(qb, kb, vb, lf, cache_fox_k, cache_fox_v, cache_fox_logf, page_table, li)
            ys = ab_post(o_a, z, o_b, ab_gdn_norm[li], ab_w_out[li])
            sl_S.append(S_new.astype(state_gdn.dtype)); sl_conv.append(buf)
            sl_k.append(kb); sl_v.append(vb); sl_f.append(lf)
        else:
            wts = (c_w_in[li], c_q_a_norm[li], c_kv_a_norm[li], c_w_q_b[li], c_q_nope_norm[li], c_q_rope_norm[li], c_k_rope_norm[li])
            qn, qr, c, kr = mla_pre(hp, pos_p, *wts)
            yp = mla_prompt(qn, qr, c, kr, c_w_uk[li], c_w_uv[li], c_k_nope_norm[li]) @ c_w_out[li]
            pl_c.append(c); pl_r.append(kr)
            qn, qr, c, kr = mla_pre(hs, pos_s, *wts)
            ys = mla_sample(qn, qr, c, kr, cache_mla_latent, cache_mla_rope, page_table, li,
                            c_w_uk[li], c_w_uv[li], c_k_nope_norm[li]) @ c_w_out[li]
            sl_c.append(c); sl_r.append(kr)
        xp = xp + yp.astype(xp.dtype)
        xs = xs + ys.astype(xs.dtype)
        hp = rms_norm(xp, norm_ffn[layer])
        hs = rms_norm(xs, norm_ffn[layer])
        tok = jnp.concatenate([hp.reshape(-1, D_MODEL), hs.reshape(-1, D_MODEL)], axis=0)
        y = hier_moe(tok, moe_w_group[layer], moe_b_group[layer], moe_w_expert[layer], moe_b_expert[layer],
                     moe_w_gate[layer], moe_w_up[layer], moe_w_down[layer])
        xp = xp + y[:b * t].reshape(xp.shape).astype(xp.dtype)
        xs = xs + y[b * t:].reshape(xs.shape).astype(xs.dtype)

    y_prompt = xp[:, N_META:]
    y_sample = xs
    p_gdn_state = jnp.stack(pl_S)
    p_gdn_conv = jnp.stack(pl_conv)
    p_fox_k = jnp.stack(pl_k)
    p_fox_v = jnp.stack(pl_v)
    p_fox_logf = jnp.stack(pl_f)
    p_mla_latent = jnp.stack(pl_c)
    p_mla_rope = jnp.stack(pl_r)
    s_gdn_state = jnp.stack(sl_S)
    s_gdn_conv = jnp.stack(sl_conv)
    s_fox_k = jnp.stack(sl_k)
    s_fox_v = jnp.stack(sl_v)
    s_fox_logf = jnp.stack(sl_f)
    s_mla_latent = jnp.stack(sl_c)
    s_mla_rope = jnp.stack(sl_r)
    return (y_prompt, y_sample, p_gdn_state, p_gdn_conv, p_fox_k, p_fox_v, p_fox_logf, p_mla_latent, p_mla_rope,
            s_gdn_state, s_gdn_conv, s_fox_k, s_fox_v, s_fox_logf, s_mla_latent, s_mla_rope)
```

```python
import functools
import math

import numpy as np
import jax
import jax.numpy as jnp
from jax import lax
from jax.experimental import pallas as pl
from jax.experimental.pallas import tpu as pltpu

F32 = jnp.float32
BF16 = jnp.bfloat16
HI = lax.Precision.HIGHEST

EPS = 1e-6
NEG_INF = -1e30
N_META = 16
LANES = 128
VMEM_LIMIT = 56 * 1024 * 1024

A_HEADS = 8
A_DK = 128
A_DV = 128
A_CONV = 4
A_CHUNK = 64
A_QK = A_HEADS * A_DK
A_V = A_HEADS * A_DV
A_CONV_CH = 2 * A_QK + A_V
B_HEADS = 8
B_KV_HEADS = 2
B_HD = 128
B_GROUP = B_HEADS // B_KV_HEADS
FOX_SCALE = B_HD ** -0.5
C_HEADS = 16
C_Q_RANK = 512
C_KV_RANK = 512
C_NOPE = 128
C_ROPE = 64
C_VD = 128
ROPE_THETA = 10000.0
MLA_SCALE = (C_NOPE + C_ROPE) ** -0.5
N_GROUPS = 8
EXPERTS_PER_GROUP = 8
N_EXPERTS = N_GROUPS * EXPERTS_PER_GROUP
TOP_K = 2
MOE_BLOCK = 128

NT_DIMS = (((1,), (1,)), ((), ()))


def _cparams(*sem):
    return pltpu.CompilerParams(dimension_semantics=sem, vmem_limit_bytes=VMEM_LIMIT)


def _round_up(n, m):
    return (n + m - 1) // m * m


def _pick_tile(n, candidates):
    for c in candidates:
        if n % c == 0:
            return c
    return n


def _bdot(a, b):
    return jnp.dot(a.astype(BF16), b.astype(BF16), preferred_element_type=F32)


def _bdot_nt(a, b):
    return lax.dot_general(a.astype(BF16), b.astype(BF16), NT_DIMS, preferred_element_type=F32)


def _hdot(a, b):
    return jnp.dot(a, b, preferred_element_type=F32, precision=HI)


def _hdot_nt(a, b):
    return lax.dot_general(a, b, NT_DIMS, preferred_element_type=F32, precision=HI)


def _mm_kernel(*refs, norm, has_res):
    it = iter(refs)
    x_ref = next(it)
    g_ref = next(it) if norm else None
    w_ref = next(it)
    r_ref = next(it) if has_res else None
    o_ref = next(it)
    xn_ref = next(it)

    @pl.when(pl.program_id(1) == 0)
    def _():
        x = x_ref[...].astype(F32)
        if norm:
            x = x * lax.rsqrt(jnp.mean(x * x, axis=-1, keepdims=True) + EPS) * g_ref[...]
        xn_ref[...] = x.astype(BF16)

    acc = jnp.dot(xn_ref[...], w_ref[...], preferred_element_type=F32)
    if has_res:
        acc = acc + r_ref[...]
    o_ref[...] = acc.astype(o_ref.dtype)


def _matmul(x, w, gain=None, residual=None, out_dtype=F32, name="matmul"):
    m, k = x.shape
    n = w.shape[1]
    tm = _pick_tile(m, (768, 512, 256, 128))
    tn = _pick_tile(n, (1024, 768, 512, 384, 256, 128))
    norm = gain is not None
    has_res = residual is not None
    args = [x]
    in_specs = [pl.BlockSpec((tm, k), lambda i, j: (i, 0))]
    if norm:
        args.append(gain.reshape(1, k).astype(F32))
        in_specs.append(pl.BlockSpec((1, k), lambda i, j: (0, 0)))
    args.append(w)
    in_specs.append(pl.BlockSpec((k, tn), lambda i, j: (0, j)))
    if has_res:
        args.append(residual)
        in_specs.append(pl.BlockSpec((tm, tn), lambda i, j: (i, j)))
    return pl.pallas_call(
        functools.partial(_mm_kernel, norm=norm, has_res=has_res),
        out_shape=jax.ShapeDtypeStruct((m, n), out_dtype),
        grid=(m // tm, n // tn),
        in_specs=in_specs,
        out_specs=pl.BlockSpec((tm, tn), lambda i, j: (i, j)),
        scratch_shapes=[pltpu.VMEM((tm, k), BF16)],
        compiler_params=_cparams("parallel", "arbitrary"),
        name=name,
    )(*args)


def _flash_kernel(*refs, scale, tq, tk, hb, shared_kv, has_bias):
    it = iter(refs)
    q_ref, k_ref, v_ref = next(it), next(it), next(it)
    b_ref = next(it) if has_bias else None
    o_ref, m_ref, l_ref, acc_ref = next(it), next(it), next(it), next(it)
    qi = pl.program_id(2)
    ki = pl.program_id(3)

    @pl.when(ki == 0)
    def _():
        m_ref[...] = jnp.full(m_ref.shape, NEG_INF, F32)
        l_ref[...] = jnp.zeros(l_ref.shape, F32)
        acc_ref[...] = jnp.zeros(acc_ref.shape, F32)

    @pl.when(ki * tk <= qi * tq + tq - 1)
    def _():
        row = qi * tq + lax.broadcasted_iota(jnp.int32, (tq, tk), 0)
        col = ki * tk + lax.broadcasted_iota(jnp.int32, (tq, tk), 1)
        visible = col <= row
        for h in range(hb):
            kh = 0 if shared_kv else h
            s = lax.dot_general(q_ref[h], k_ref[kh], NT_DIMS, preferred_element_type=F32) * scale
            if has_bias:
                s = s + b_ref[h]
            s = jnp.where(visible, s, NEG_INF)
            m_prev = m_ref[h]
            m_new = jnp.maximum(m_prev, jnp.max(s, axis=-1, keepdims=True))
            p = jnp.exp(s - m_new)
            alpha = jnp.exp(m_prev - m_new)
            l_ref[h] = alpha * l_ref[h] + jnp.sum(p, axis=-1, keepdims=True)
            acc_ref[h] = alpha * acc_ref[h] + jnp.dot(p.astype(BF16), v_ref[kh], preferred_element_type=F32)
            m_ref[h] = m_new

    @pl.when(ki == pl.num_programs(3) - 1)
    def _():
        o_ref[...] = (acc_ref[...] / l_ref[...]).astype(o_ref.dtype)


def _flash_attention(q, k, v, kbias, scale, hb, tq, tk, name):
    b, h, tp, dq = q.shape
    hk = k.shape[1]
    dv = v.shape[-1]
    shared_kv = hk != h
    if shared_kv:
        assert hk * hb == h
    nq, nk = tp // tq, tp // tk
    kvb = 1 if shared_kv else hb

    def kv_map(bi, hg, qi, ki):
        return (bi, hg, jnp.minimum(ki, (qi * tq + tq - 1) // tk), 0)

    args = [q, k, v]
    in_specs = [
        pl.BlockSpec((None, hb, tq, dq), lambda bi, hg, qi, ki: (bi, hg, qi, 0)),
        pl.BlockSpec((None, kvb, tk, dq), kv_map),
        pl.BlockSpec((None, kvb, tk, dv), kv_map),
    ]
    has_bias = kbias is not None
    if has_bias:
        args.append(kbias)
        in_specs.append(pl.BlockSpec((None, hb, 1, tk),
                                     lambda bi, hg, qi, ki: (bi, hg, 0, jnp.minimum(ki, (qi * tq + tq - 1) // tk))))
    return pl.pallas_call(
        functools.partial(_flash_kernel, scale=scale, tq=tq, tk=tk, hb=hb, shared_kv=shared_kv, has_bias=has_bias),
        out_shape=jax.ShapeDtypeStruct((b, h, tp, dv), F32),
        grid=(b, h // hb, nq, nk),
        in_specs=in_specs,
        out_specs=pl.BlockSpec((None, hb, tq, dv), lambda bi, hg, qi, ki: (bi, hg, qi, 0)),
        scratch_shapes=[pltpu.VMEM((hb, tq, 1), F32), pltpu.VMEM((hb, tq, 1), F32), pltpu.VMEM((hb, tq, dv), F32)],
        compiler_params=_cparams("parallel", "parallel", "parallel", "arbitrary"),
        name=name,
    )(*args)


def _unit_lower_inverse(a):
    n = a.shape[0]
    eye = (lax.broadcasted_iota(jnp.int32, (n, n), 0) == lax.broadcasted_iota(jnp.int32, (n, n), 1)).astype(F32)
    x = eye - a
    p = _hdot(a, a)
    covered = 2
    while covered < n:
        x = x + _hdot(x, p)
        covered *= 2
        if covered < n:
            p = _hdot(p, p)
    return x


def _gdn_chunk_kernel(q_ref, k_ref, v_ref, gt_ref, o_ref, sout_ref, s_ref, *, chunk, heads):
    c = pl.program_id(1)

    @pl.when(c == 0)
    def _():
        s_ref[...] = jnp.zeros(s_ref.shape, F32)

    gates = gt_ref[...]
    r = lax.broadcasted_iota(jnp.int32, (chunk, chunk), 0)
    cidx = lax.broadcasted_iota(jnp.int32, (chunk, chunk), 1)
    incl = r >= cidx
    strict = r > cidx
    gcum = _hdot(incl.astype(F32), gates)
    eye = (lax.broadcasted_iota(jnp.int32, (LANES, LANES), 0)
           == lax.broadcasted_iota(jnp.int32, (LANES, LANES), 1)).astype(F32)
    gcum_t = _hdot_nt(eye, gcum)
    for h in range(heads):
        sl = slice(h * A_DK, (h + 1) * A_DK)
        q = q_ref[:, sl]
        k = k_ref[:, sl]
        v = v_ref[:, sl]
        g_col = gcum[:, h:h + 1]
        g_row = gcum_t[h:h + 1, :]
        beta = gates[:, heads + h:heads + h + 1]
        g_last = gcum[chunk - 1:chunk, h:h + 1]
        decay = jnp.where(incl, jnp.exp(jnp.where(incl, g_col - g_row, 0.0)), 0.0)
        e_g = jnp.exp(g_col)
        kbeta = k * beta
        a = jnp.where(strict, _bdot_nt(kbeta, k) * decay, 0.0)
        t_inv = _unit_lower_inverse(a)
        wu = _hdot(t_inv, jnp.concatenate([kbeta * e_g, v * beta], axis=1))
        w = wu[:, :A_DK]
        u = wu[:, A_DK:]
        att = _bdot_nt(q, k) * decay
        s = s_ref[h]
        v_new = u - _bdot(w, s)
        o_ref[:, sl] = _bdot(q * e_g, s) + _bdot(att, v_new)
        kg = k * jnp.exp(g_last - g_col)
        s_ref[h] = s * jnp.exp(g_last) + lax.dot_general(
            kg.astype(BF16), v_new.astype(BF16), (((0,), (0,)), ((), ())), preferred_element_type=F32)

    @pl.when(c == pl.num_programs(1) - 1)
    def _():
        sout_ref[...] = s_ref[...]


def _gdn_chunked(q, k, v, gates):
    b, tp, _ = q.shape
    nc = tp // A_CHUNK
    tok_spec = pl.BlockSpec((None, A_CHUNK, A_QK), lambda bi, ci: (bi, ci, 0))
    return pl.pallas_call(
        functools.partial(_gdn_chunk_kernel, chunk=A_CHUNK, heads=A_HEADS),
        out_shape=(jax.ShapeDtypeStruct((b, tp, A_V), F32),
                   jax.ShapeDtypeStruct((b, A_HEADS, A_DK, A_DV), F32)),
        grid=(b, nc),
        in_specs=[tok_spec, tok_spec, tok_spec,
                  pl.BlockSpec((None, A_CHUNK, LANES), lambda bi, ci: (bi, ci, 0))],
        out_specs=(pl.BlockSpec((None, A_CHUNK, A_V), lambda bi, ci: (bi, ci, 0)),
                   pl.BlockSpec((None, A_HEADS, A_DK, A_DV), lambda bi, ci: (bi, 0, 0, 0))),
        scratch_shapes=[pltpu.VMEM((A_HEADS, A_DK, A_DV), F32)],
        compiler_params=_cparams("parallel", "arbitrary"),
        name="gdn_chunked",
    )(q, k, v, gates)


def _gdn_step_kernel(s_ref, q_ref, k_ref, v_ref, gt_ref, o_ref, sout_ref, *, sb, heads):
    eye = (lax.broadcasted_iota(jnp.int32, (LANES, LANES), 0)
           == lax.broadcasted_iota(jnp.int32, (LANES, LANES), 1)).astype(F32)
    zpad = jnp.zeros((LANES - heads, A_DK), F32)
    for i in range(sb):
        k8 = k_ref[i]
        q8 = q_ref[i]
        v8 = v_ref[i]
        k_t = _hdot_nt(eye, jnp.concatenate([k8, zpad], axis=0))
        q_t = _hdot_nt(eye, jnp.concatenate([q8, zpad], axis=0))
        outs = []
        for h in range(heads):
            decay = jnp.exp(gt_ref[i:i + 1, h:h + 1])
            beta = gt_ref[i:i + 1, heads + h:heads + h + 1]
            k_col = k_t[:, h:h + 1]
            s = s_ref[i, h] * decay
            pred = jnp.sum(k_col * s, axis=0, keepdims=True)
            delta = (v8[h:h + 1, :] - pred) * beta
            s = s + k_col * delta
            sout_ref[i, h] = s
            outs.append(jnp.sum(q_t[:, h:h + 1] * s, axis=0, keepdims=True))
        o_ref[i] = jnp.concatenate(outs, axis=0)


def _gdn_step(state, q, k, v, gates):
    n = state.shape[0]
    sb = 8
    vec_spec = pl.BlockSpec((sb, A_HEADS, A_DK), lambda i: (i, 0, 0))
    st_spec = pl.BlockSpec((sb, A_HEADS, A_DK, A_DV), lambda i: (i, 0, 0, 0))
    return pl.pallas_call(
        functools.partial(_gdn_step_kernel, sb=sb, heads=A_HEADS),
        out_shape=(jax.ShapeDtypeStruct((n, A_HEADS, A_DV), F32),
                   jax.ShapeDtypeStruct(state.shape, F32)),
        grid=(n // sb,),
        in_specs=[st_spec, vec_spec, vec_spec, vec_spec, pl.BlockSpec((sb, LANES), lambda i: (i, 0))],
        out_specs=(vec_spec, st_spec),
        compiler_params=_cparams("parallel"),
        name="gdn_step",
    )(state, q, k, v, gates)


def _fox_sample_kernel(pt_ref, q_ref, knew_ref, vnew_ref, lfnew_ref, *refs, pps, scale):
    k_refs = refs[:pps]
    v_refs = refs[pps:2 * pps]
    lf_refs = refs[2 * pps:3 * pps]
    o_ref, m_ref, l_ref, acc_ref, carry_ref = refs[3 * pps:]
    j = pl.program_id(1)
    q = q_ref[...]
    row = lax.broadcasted_iota(jnp.int32, (B_HEADS, 1), 0)

    @pl.when(j == 0)
    def _():
        m_ref[...] = jnp.sum(q * knew_ref[...], axis=-1, keepdims=True) * scale
        l_ref[...] = jnp.ones(l_ref.shape, F32)
        acc_ref[...] = vnew_ref[...]
        carry_ref[...] = lfnew_ref[...]

    qb = q.astype(BF16)
    suffix = (lax.broadcasted_iota(jnp.int32, (LANES, LANES), 0)
              > lax.broadcasted_iota(jnp.int32, (LANES, LANES), 1)).astype(F32)
    carry = carry_ref[...]
    s_parts, v0_parts, v1_parts = [], [], []
    for p in range(pps):
        lf_t = lf_refs[p][...]
        within = _hdot(lf_t, suffix)
        kp = k_refs[p]
        s0 = _bdot_nt(qb, kp[:, 0, :])
        s1 = _bdot_nt(qb, kp[:, 1, :])
        s = jnp.where(row < B_GROUP, s0, s1) * scale + within + carry
        carry = carry + jnp.sum(lf_t, axis=-1, keepdims=True)
        s_parts.append(s)
        v0_parts.append(v_refs[p][:, 0, :])
        v1_parts.append(v_refs[p][:, 1, :])
    carry_ref[...] = carry
    s = jnp.concatenate(s_parts, axis=1)
    m_prev = m_ref[...]
    m_new = jnp.maximum(m_prev, jnp.max(s, axis=-1, keepdims=True))
    pr = jnp.exp(s - m_new)
    alpha = jnp.exp(m_prev - m_new)
    l_ref[...] = alpha * l_ref[...] + jnp.sum(pr, axis=-1, keepdims=True)
    prb = pr.astype(BF16)
    o0 = _bdot(prb, jnp.concatenate(v0_parts, axis=0))
    o1 = _bdot(prb, jnp.concatenate(v1_parts, axis=0))
    acc_ref[...] = alpha * acc_ref[...] + jnp.where(row < B_GROUP, o0, o1)
    m_ref[...] = m_new

    @pl.when(j == pl.num_programs(1) - 1)
    def _():
        o_ref[...] = acc_ref[...] / l_ref[...]


def _fox_sample(q, k_new, v_new, lf_new, cache_k, cache_v, cache_lf_t, page_table, li):
    n, n_pages = page_table.shape
    page = cache_k.shape[2]
    assert page == LANES
    pps = 8
    assert n_pages % pps == 0
    steps = n_pages // pps
    pt_flat = page_table.reshape(-1)

    def page_of(bi, j, pt, p):
        return pt[bi * n_pages + (n_pages - 1 - (j * pps + p))]

    head_spec = pl.BlockSpec((None, B_HEADS, B_HD), lambda bi, j, pt: (bi, 0, 0))
    in_specs = [head_spec, head_spec, head_spec,
                pl.BlockSpec((None, B_HEADS, 1), lambda bi, j, pt: (bi, 0, 0))]
    args = [q, k_new, v_new, lf_new]
    for cache in (cache_k, cache_v):
        for p in range(pps):
            in_specs.append(pl.BlockSpec((None, None, page, B_KV_HEADS, B_HD),
                                         functools.partial(lambda bi, j, pt, p: (li, page_of(bi, j, pt, p), 0, 0, 0), p=p)))
            args.append(cache)
    for p in range(pps):
        in_specs.append(pl.BlockSpec((None, None, B_HEADS, page),
                                     functools.partial(lambda bi, j, pt, p: (li, page_of(bi, j, pt, p), 0, 0), p=p)))
        args.append(cache_lf_t)
    grid_spec = pltpu.PrefetchScalarGridSpec(
        num_scalar_prefetch=1,
        grid=(n, steps),
        in_specs=in_specs,
        out_specs=pl.BlockSpec((None, B_HEADS, B_HD), lambda bi, j, pt: (bi, 0, 0)),
        scratch_shapes=[pltpu.VMEM((B_HEADS, 1), F32), pltpu.VMEM((B_HEADS, 1), F32),
                        pltpu.VMEM((B_HEADS, B_HD), F32), pltpu.VMEM((B_HEADS, 1), F32)],
    )
    return pl.pallas_call(
        functools.partial(_fox_sample_kernel, pps=pps, scale=FOX_SCALE),
        out_shape=jax.ShapeDtypeStruct((n, B_HEADS, B_HD), F32),
        grid_spec=grid_spec,
        compiler_params=_cparams("parallel", "arbitrary"),
        name="fox_sample",
    )(pt_flat, *args)


def _mla_sample_kernel(pt_ref, qn_ref, qr_ref, gain_ref, cnew_ref, rnew_ref, wukt_ref, wuv_ref, *refs, pps, scale):
    lat_refs = refs[:pps]
    rope_refs = refs[pps:2 * pps]
    o_ref, wq_ref, m_ref, l_ref, acc_ref = refs[2 * pps:]
    bi = pl.program_id(0)
    j = pl.program_id(1)
    nrow = C_HEADS * C_NOPE
    head_of_col = lax.broadcasted_iota(jnp.int32, (C_HEADS, nrow), 1) // C_NOPE
    own = head_of_col == lax.broadcasted_iota(jnp.int32, (C_HEADS, nrow), 0)

    @pl.when((bi == 0) & (j == 0))
    def _():
        wq_ref[0:nrow, :] = wukt_ref[...]

    def attend(cc, rr, valid):
        proj = lax.dot_general(wq_ref[...], cc, NT_DIMS, preferred_element_type=F32)
        kp = proj[0:nrow].reshape(C_HEADS, C_NOPE, cc.shape[0])
        rs = lax.rsqrt(jnp.sum(kp * kp, axis=1) * (1.0 / C_NOPE) + EPS)
        s = (proj[nrow:nrow + C_HEADS] * rs + _bdot_nt(qr_ref[...], rr)) * scale
        if valid is not None:
            s = jnp.where(valid, s, NEG_INF)
        m_prev = m_ref[...]
        m_new = jnp.maximum(m_prev, jnp.max(s, axis=-1, keepdims=True))
        p = jnp.exp(s - m_new)
        alpha = jnp.exp(m_prev - m_new)
        l_ref[...] = alpha * l_ref[...] + jnp.sum(p, axis=-1, keepdims=True)
        acc_ref[...] = alpha * acc_ref[...] + _bdot(p, cc)
        m_ref[...] = m_new

    @pl.when(j == 0)
    def _():
        qg = jnp.concatenate([qn_ref[...]] * C_HEADS, axis=1) * gain_ref[...]
        qbd = jnp.where(own, qg, 0.0)
        wq_ref[nrow:nrow + C_HEADS, :] = _bdot(qbd, wukt_ref[...]).astype(BF16)
        m_ref[...] = jnp.full(m_ref.shape, NEG_INF, F32)
        l_ref[...] = jnp.zeros(l_ref.shape, F32)
        acc_ref[...] = jnp.zeros(acc_ref.shape, F32)
        first = lax.broadcasted_iota(jnp.int32, (1, LANES), 1) == 0
        attend(cnew_ref[...].astype(BF16), rnew_ref[...].astype(BF16), first)

    for p in range(0, pps, 2):
        cc = jnp.concatenate([lat_refs[p][...], lat_refs[p + 1][...]], axis=0).astype(BF16)
        rr = jnp.concatenate([rope_refs[p][...], rope_refs[p + 1][...]], axis=0).astype(BF16)
        attend(cc, rr, None)

    @pl.when(j == pl.num_programs(1) - 1)
    def _():
        o_lat = acc_ref[...] / l_ref[...]
        full = _bdot(o_lat, wuv_ref[...])
        o_ref[...] = jnp.sum(jnp.where(own, full, 0.0), axis=0, keepdims=True)


def _mla_sample(q_nope, q_rope, k_gain, c_new, r_new, w_uk_t, w_uv, cache_lat, cache_rope, page_table, li):
    n, n_pages = page_table.shape
    page = cache_lat.shape[2]
    assert page == LANES
    pps = 8
    assert n_pages % pps == 0
    steps = n_pages // pps
    pt_flat = page_table.reshape(-1)
    nrow = C_HEADS * C_NOPE
    gain_t = jnp.tile(k_gain.astype(F32), C_HEADS).reshape(1, nrow)

    def page_of(bi, j, pt, p):
        return pt[bi * n_pages + j * pps + p]

    in_specs = [
        pl.BlockSpec((None, C_HEADS, C_NOPE), lambda bi, j, pt: (bi, 0, 0)),
        pl.BlockSpec((None, C_HEADS, C_ROPE), lambda bi, j, pt: (bi, 0, 0)),
        pl.BlockSpec((1, nrow), lambda bi, j, pt: (0, 0)),
        pl.BlockSpec((None, page, C_KV_RANK), lambda bi, j, pt: (bi, 0, 0)),
        pl.BlockSpec((None, page, C_ROPE), lambda bi, j, pt: (bi, 0, 0)),
        pl.BlockSpec((nrow, C_KV_RANK), lambda bi, j, pt: (0, 0)),
        pl.BlockSpec((C_KV_RANK, C_HEADS * C_VD), lambda bi, j, pt: (0, 0)),
    ]
    args = [q_nope, q_rope, gain_t, c_new, r_new, w_uk_t, w_uv]
    for cache, width in ((cache_lat, C_KV_RANK), (cache_rope, C_ROPE)):
        for p in range(pps):
            in_specs.append(pl.BlockSpec((None, None, page, width),
                                         functools.partial(lambda bi, j, pt, p: (li, page_of(bi, j, pt, p), 0, 0), p=p)))
            args.append(cache)
    grid_spec = pltpu.PrefetchScalarGridSpec(
        num_scalar_prefetch=1,
        grid=(n, steps),
        in_specs=in_specs,
        out_specs=pl.BlockSpec((None, 1, C_HEADS * C_VD), lambda bi, j, pt: (bi, 0, 0)),
        scratch_shapes=[pltpu.VMEM((nrow + C_HEADS, C_KV_RANK), BF16),
                        pltpu.VMEM((C_HEADS, 1), F32), pltpu.VMEM((C_HEADS, 1), F32),
                        pltpu.VMEM((C_HEADS, C_KV_RANK), F32)],
    )
    return pl.pallas_call(
        functools.partial(_mla_sample_kernel, pps=pps, scale=MLA_SCALE),
        out_shape=jax.ShapeDtypeStruct((n, 1, C_HEADS * C_VD), F32),
        grid_spec=grid_spec,
        compiler_params=_cparams("arbitrary", "arbitrary"),
        name="mla_sample",
    )(pt_flat, *args)


def _router_kernel(x_ref, g_ref, w_ref, h_ref, lg_ref):
    x = x_ref[...]
    xn = x * lax.rsqrt(jnp.mean(x * x, axis=-1, keepdims=True) + EPS) * g_ref[...]
    h_ref[...] = xn.astype(h_ref.dtype)
    lg_ref[...] = _hdot(xn, w_ref[...])


def _router(x, gain, w_router):
    m, d = x.shape
    tm = _pick_tile(m, (256, 128))
    return pl.pallas_call(
        _router_kernel,
        out_shape=(jax.ShapeDtypeStruct((m, d), BF16), jax.ShapeDtypeStruct((m, LANES), F32)),
        grid=(m // tm,),
        in_specs=[pl.BlockSpec((tm, d), lambda i: (i, 0)),
                  pl.BlockSpec((1, d), lambda i: (0, 0)),
                  pl.BlockSpec((d, LANES), lambda i: (0, 0))],
        out_specs=(pl.BlockSpec((tm, d), lambda i: (i, 0)), pl.BlockSpec((tm, LANES), lambda i: (i, 0))),
        compiler_params=_cparams("parallel"),
        name="moe_router",
    )(x, gain.reshape(1, d).astype(F32), w_router)


def _expert_kernel(be_ref, na_ref, x_ref, wg_ref, wu_ref, wd_ref, o_ref):
    i = pl.program_id(0)

    @pl.when(i < na_ref[0])
    def _():
        x = x_ref[...]
        gate = jnp.dot(x, wg_ref[...].astype(BF16), preferred_element_type=F32)
        up = jnp.dot(x, wu_ref[...].astype(BF16), preferred_element_type=F32)
        hidden = gate * jax.nn.sigmoid(gate) * up
        o_ref[...] = jnp.dot(hidden.astype(BF16), wd_ref[...].astype(BF16), preferred_element_type=F32)

    @pl.when(i >= na_ref[0])
    def _():
        o_ref[...] = jnp.zeros(o_ref.shape, F32)


def _experts(xs, block_expert, n_active, w_gate, w_up, w_down):
    n_slots, d = xs.shape
    n_blocks = n_slots // MOE_BLOCK
    de = w_gate.shape[-1]
    grid_spec = pltpu.PrefetchScalarGridSpec(
        num_scalar_prefetch=2,
        grid=(n_blocks,),
        in_specs=[pl.BlockSpec((MOE_BLOCK, d), lambda i, be, na: (i, 0)),
                  pl.BlockSpec((None, d, de), lambda i, be, na: (be[i], 0, 0)),
                  pl.BlockSpec((None, d, de), lambda i, be, na: (be[i], 0, 0)),
                  pl.BlockSpec((None, de, d), lambda i, be, na: (be[i], 0, 0))],
        out_specs=pl.BlockSpec((MOE_BLOCK, d), lambda i, be, na: (i, 0)),
    )
    return pl.pallas_call(
        _expert_kernel,
        out_shape=jax.ShapeDtypeStruct((n_slots, d), F32),
        grid_spec=grid_spec,
        compiler_params=_cparams("arbitrary"),
        name="moe_experts",
    )(block_expert, n_active, xs, w_gate, w_up, w_down)


def _hier_moe(x, n_valid, gain, w_group, b_group, w_expert, b_expert, w_gate, w_up, w_down):
    m, d = x.shape
    w_router = jnp.zeros((d, LANES), F32).at[:, :N_GROUPS].set(w_group).at[:, N_GROUPS:N_GROUPS + N_EXPERTS].set(w_expert)
    h, logits = _router(x, gain, w_router)
    n = n_valid
    grp_logit = logits[:n, :N_GROUPS] + b_group
    grp_prob = jax.nn.softmax(grp_logit, axis=-1)
    grp_p, grp_i = lax.top_k(grp_prob, 1)
    e_logits = (logits[:n, N_GROUPS:N_GROUPS + N_EXPERTS] + b_expert).reshape(n, N_GROUPS, EXPERTS_PER_GROUP)
    e_in = jnp.take_along_axis(e_logits, grp_i[:, :, None], axis=1)[:, 0]
    top_p, top_i = lax.top_k(jax.nn.softmax(e_in, axis=-1), TOP_K)
    gate = grp_p * top_p / jnp.sum(top_p, axis=-1, keepdims=True)
    expert = grp_i * EXPERTS_PER_GROUP + top_i
    nk = n * TOP_K
    flat_e = expert.reshape(-1)
    flat_tok = jnp.repeat(jnp.arange(n, dtype=jnp.int32), TOP_K)
    order = jnp.argsort(flat_e)
    se, stok = flat_e[order], flat_tok[order]
    counts = jnp.zeros((N_EXPERTS,), jnp.int32).at[flat_e].add(1)
    padded = (counts + MOE_BLOCK - 1) // MOE_BLOCK * MOE_BLOCK
    pad_end = jnp.cumsum(padded)
    pad_start = pad_end - padded
    seg_start = jnp.cumsum(counts) - counts
    dest = pad_start[se] + jnp.arange(nk, dtype=jnp.int32) - seg_start[se]
    n_slots = _round_up(nk + N_EXPERTS * (MOE_BLOCK - 1), MOE_BLOCK)
    n_blocks = n_slots // MOE_BLOCK
    slot_tok = jnp.zeros((n_slots,), jnp.int32).at[dest].set(stok)
    n_active = (pad_end[-1] // MOE_BLOCK).astype(jnp.int32).reshape(1)
    block_expert = jnp.searchsorted(pad_end, jnp.arange(n_blocks, dtype=jnp.int32) * MOE_BLOCK, side='right')
    block_expert = jnp.minimum(block_expert, N_EXPERTS - 1).astype(jnp.int32)
    last_expert = block_expert[jnp.maximum(n_active[0] - 1, 0)]
    block_expert = jnp.where(jnp.arange(n_blocks) < n_active[0], block_expert, last_expert)
    xs = h[slot_tok]
    ys = _experts(xs, block_expert, n_active, w_gate, w_up, w_down)
    slot_of = jnp.zeros((nk,), jnp.int32).at[order].set(dest).reshape(n, TOP_K)
    y = jnp.sum(ys[slot_of] * gate[:, :, None], axis=1)
    return x.at[:n].add(y)


def _rms(x, g):
    xf = x.astype(F32)
    return xf * lax.rsqrt(jnp.mean(xf * xf, axis=-1, keepdims=True) + EPS) * g.astype(F32)


def _l2(x):
    return x * lax.rsqrt(jnp.sum(x * x, axis=-1, keepdims=True) + EPS)


def _rope(x, pos):
    half = C_ROPE // 2
    inv = ROPE_THETA ** (-jnp.arange(half, dtype=F32) / half)
    ang = pos.astype(F32)[:, None] * inv[None, :]
    cos = jnp.cos(ang)[None, :, None, :]
    sin = jnp.sin(ang)[None, :, None, :]
    x1, x2 = x[..., :half], x[..., half:]
    return jnp.concatenate([x1 * cos - x2 * sin, x2 * cos + x1 * sin], axis=-1)


_AB_WIDE = A_CONV_CH + A_V + B_HEADS * B_HD + 2 * B_KV_HEADS * B_HD
_AB_SMALL = LANES


def _pack_ab_w_in(w):
    sizes = (A_CONV_CH, A_V, A_HEADS, A_HEADS, B_HEADS * B_HD, B_KV_HEADS * B_HD, B_KV_HEADS * B_HD, B_HEADS)
    qkv, z, a_raw, b_raw, q_b, k_b, v_b, f_raw = jnp.split(w, np.cumsum(sizes)[:-1].tolist(), axis=-1)
    small = jnp.concatenate([a_raw, b_raw, f_raw], axis=-1)
    small = jnp.pad(small, ((0, 0), (0, _AB_SMALL - small.shape[-1])))
    packed = jnp.concatenate([qkv, z, q_b, k_b, v_b, small], axis=-1)
    n_pad = _round_up(packed.shape[-1], 768)
    return jnp.pad(packed, ((0, 0), (0, n_pad - packed.shape[-1]))).astype(BF16)


def _ab_pre(proj, conv_buf, conv_w, a_log, dt_bias, fgate_bias, q_norm, k_norm):
    b, t, _ = proj.shape
    o = 0
    qkv_a = proj[..., o:o + A_CONV_CH]; o += A_CONV_CH
    z = proj[..., o:o + A_V]; o += A_V
    q_b = proj[..., o:o + B_HEADS * B_HD]; o += B_HEADS * B_HD
    k_b = proj[..., o:o + B_KV_HEADS * B_HD]; o += B_KV_HEADS * B_HD
    v_b = proj[..., o:o + B_KV_HEADS * B_HD]; o += B_KV_HEADS * B_HD
    a_raw = proj[..., o:o + A_HEADS]
    b_raw = proj[..., o + A_HEADS:o + 2 * A_HEADS]
    f_raw = proj[..., o + 2 * A_HEADS:o + 2 * A_HEADS + B_HEADS]
    xp = jnp.concatenate([conv_buf.astype(F32), qkv_a], axis=1)
    conv = xp[:, 0:t] * conv_w[0]
    for i in range(1, A_CONV):
        conv = conv + xp[:, i:i + t] * conv_w[i]
    conv = jax.nn.silu(conv)
    new_buf = xp[:, t:]
    qa = _l2(conv[..., :A_QK].reshape(b, t, A_HEADS, A_DK)) * (A_DK ** -0.5)
    ka = _l2(conv[..., A_QK:2 * A_QK].reshape(b, t, A_HEADS, A_DK))
    va = conv[..., 2 * A_QK:].reshape(b, t, A_HEADS, A_DV)
    beta = jax.nn.sigmoid(b_raw)
    g = -jnp.exp(a_log.astype(F32)) * jax.nn.softplus(a_raw + dt_bias.astype(F32))
    qb = _rms(q_b.reshape(b, t, B_HEADS, B_HD), q_norm)
    kb = _rms(k_b.reshape(b, t, B_KV_HEADS, B_HD), k_norm)
    vb = v_b.reshape(b, t, B_KV_HEADS, B_HD)
    logf = jax.nn.log_sigmoid(f_raw + fgate_bias.astype(F32))
    return (qa, ka, va, g, beta, z, new_buf), (qb, kb, vb, logf)


def _gate_lanes(g, beta):
    both = jnp.concatenate([g, beta], axis=-1)
    return jnp.pad(both, [(0, 0)] * (both.ndim - 1) + [(0, LANES - both.shape[-1])])


def _pad_seq(a, tp):
    return jnp.pad(a, [(0, 0), (0, tp - a.shape[1])] + [(0, 0)] * (a.ndim - 2))


def _heads_first(a, tp):
    return jnp.swapaxes(_pad_seq(a, tp), 1, 2).astype(BF16)


def _gdn_gated_out(o_gdn, z, gdn_norm):
    b, t = o_gdn.shape[:2]
    og = _rms(o_gdn, gdn_norm) * jax.nn.silu(z.reshape(b, t, A_HEADS, A_DV))
    return og.reshape(b, t, A_V)


def kernel(x_prompt, x_sample, state_gdn, state_gdn_conv, cache_fox_k, cache_fox_v, cache_fox_logf, cache_mla_latent, cache_mla_rope, page_table, meta_tokens, norm_mix, norm_ffn, ab_w_in, ab_conv_w, ab_a_log, ab_dt_bias, ab_gdn_norm, ab_fgate_bias, ab_q_norm, ab_k_norm, ab_w_out, c_w_in, c_q_a_norm, c_kv_a_norm, c_w_q_b, c_w_uk, c_w_uv, c_q_nope_norm, c_q_rope_norm, c_k_nope_norm, c_k_rope_norm, c_w_out, moe_w_group, moe_b_group, moe_w_expert, moe_b_expert, moe_w_gate, moe_w_up, moe_w_down):
    b, seq, d = x_prompt.shape
    ns, ds, _ = x_sample.shape
    assert ds == 1
    depth = norm_mix.shape[0]
    t = seq + N_META
    n_p = b * t
    n_tok = n_p + ns
    m_rows = _round_up(n_tok, 256)
    past = page_table.shape[1] * cache_fox_k.shape[2]
    tp_attn = _round_up(t, 256)
    tp_gdn = _round_up(t, A_CHUNK)

    meta = jnp.broadcast_to(meta_tokens.astype(F32)[None], (b, N_META, d))
    xp0 = jnp.concatenate([meta, x_prompt], axis=1).reshape(n_p, d)
    x = jnp.concatenate([xp0, x_sample.reshape(ns, d), jnp.zeros((m_rows - n_tok, d), F32)], axis=0)

    pos_p = jnp.arange(t)
    pos_s = past + jnp.arange(ds)
    outs = {name: [] for name in ("pS", "pconv", "pk", "pv", "pf", "pc", "pr", "sS", "sconv", "sk", "sv", "sf", "sc", "sr")}

    for layer in range(depth):
        li = layer // 2
        if layer % 2 == 0:
            proj = _matmul(x, _pack_ab_w_in(ab_w_in[li]), gain=norm_mix[layer], name="ab_in_proj")
            wts = (ab_conv_w[li], ab_a_log[li], ab_dt_bias[li], ab_fgate_bias[li], ab_q_norm[li], ab_k_norm[li])
            (qa, ka, va, g, beta, z, buf), (qb, kb, vb, lf) = _ab_pre(
                proj[:n_p].reshape(b, t, -1), jnp.zeros((b, A_CONV - 1, A_CONV_CH), F32), *wts)
            o_a, s_new = _gdn_chunked(_pad_seq(qa.reshape(b, t, A_QK), tp_gdn), _pad_seq(ka.reshape(b, t, A_QK), tp_gdn),
                                      _pad_seq(va.reshape(b, t, A_V), tp_gdn), _pad_seq(_gate_lanes(g, beta), tp_gdn))
            o_a = o_a[:, :t].reshape(b, t, A_HEADS, A_DV)
            f_cum = jnp.cumsum(lf, axis=1)
            kbias = jnp.swapaxes(_pad_seq(-f_cum, tp_attn), 1, 2)[:, :, None, :]
            o_b = _flash_attention(_heads_first(qb, tp_attn), _heads_first(kb, tp_attn), _heads_first(vb, tp_attn),
                                   kbias, FOX_SCALE, hb=B_GROUP, tq=256, tk=256, name="fox_prompt")
            o_b = jnp.swapaxes(o_b, 1, 2)[:, :t].reshape(b, t, B_HEADS * B_HD)
            mixed_p = jnp.concatenate([_gdn_gated_out(o_a, z, ab_gdn_norm[li]), o_b], axis=-1)
            outs["pS"].append(s_new); outs["pconv"].append(buf)
            outs["pk"].append(kb); outs["pv"].append(vb); outs["pf"].append(lf)
            (qa, ka, va, g, beta, z, buf), (qb, kb, vb, lf) = _ab_pre(
                proj[n_p:n_tok].reshape(ns, 1, -1), state_gdn_conv[li], *wts)
            o_a, s_new = _gdn_step(state_gdn[li], qa[:, 0], ka[:, 0], va[:, 0], _gate_lanes(g[:, 0], beta[:, 0]))
            rep = lambda a: jnp.repeat(a[:, 0], B_GROUP, axis=1)
            o_b = _fox_sample(qb[:, 0], rep(kb), rep(vb), lf[:, 0, :, None], cache_fox_k, cache_fox_v,
                              jnp.swapaxes(cache_fox_logf, 2, 3), page_table, li)
            mixed_s = jnp.concatenate([_gdn_gated_out(o_a[:, None], z, ab_gdn_norm[li]),
                                       o_b.reshape(ns, 1, B_HEADS * B_HD)], axis=-1)
            outs["sS"].append(s_new); outs["sconv"].append(buf)
            outs["sk"].append(kb); outs["sv"].append(vb); outs["sf"].append(lf)
            mixed = jnp.concatenate([mixed_p.reshape(n_p, -1), mixed_s.reshape(ns, -1),
                                     jnp.zeros((m_rows - n_tok, mixed_p.shape[-1]), F32)], axis=0)
            x = _matmul(mixed, ab_w_out[li].astype(BF16), residual=x, name="ab_out_proj")
        else:
            c_in = c_w_in[li]
            n_in = _round_up(c_in.shape[1], LANES)
            lat = _matmul(x, jnp.pad(c_in, ((0, 0), (0, n_in - c_in.shape[1]))).astype(BF16),
                          gain=norm_mix[layer], name="mla_in_proj")
            q_lat, c, kr = lat[:, :C_Q_RANK], lat[:, C_Q_RANK:C_Q_RANK + C_KV_RANK], lat[:, C_Q_RANK + C_KV_RANK:C_Q_RANK + C_KV_RANK + C_ROPE]
            q = _matmul(q_lat, c_w_q_b[li].reshape(C_Q_RANK, -1).astype(BF16), gain=c_q_a_norm[li], name="mla_q_proj")
            q = q.reshape(m_rows, C_HEADS, C_NOPE + C_ROPE)
            q_nope = _rms(q[..., :C_NOPE], c_q_nope_norm[li])
            q_rope_n = _rms(q[..., C_NOPE:], c_q_rope_norm[li])
            c = _rms(c, c_kv_a_norm[li])
            kr_n = _rms(kr, c_k_rope_norm[li])
            w_uk = c_w_uk[li].reshape(C_KV_RANK, -1)
            w_uv = c_w_uv[li].reshape(C_KV_RANK, -1)
            qn_p = q_nope[:n_p].reshape(b, t, C_HEADS, C_NOPE)
            qr_p = _rope(q_rope_n[:n_p].reshape(b, t, C_HEADS, C_ROPE), pos_p)
            c_p = c[:n_p].reshape(b, t, C_KV_RANK)
            kr_p = _rope(kr_n[:n_p].reshape(b, t, 1, C_ROPE), pos_p)[:, :, 0]
            c_rows = jnp.pad(c[:n_p], ((0, _round_up(n_p, 256) - n_p), (0, 0)))
            kv = _matmul(c_rows, jnp.concatenate([w_uk, w_uv], axis=1).astype(BF16), name="mla_kv_proj")[:n_p]
            k_nope = _rms(kv[:, :C_HEADS * C_NOPE].reshape(b, t, C_HEADS, C_NOPE), c_k_nope_norm[li])
            v_p = kv[:, C_HEADS * C_NOPE:].reshape(b, t, C_HEADS, C_VD)
            zpad = jnp.zeros((b, t, C_HEADS, C_NOPE - C_ROPE), F32)
            q_full = jnp.concatenate([qn_p, qr_p, zpad], axis=-1)
            k_full = jnp.concatenate([k_nope, jnp.broadcast_to(kr_p[:, :, None, :], (b, t, C_HEADS, C_ROPE)), zpad], axis=-1)
            o_p = _flash_attention(_heads_first(q_full, tp_attn), _heads_first(k_full, tp_attn), _heads_first(v_p, tp_attn),
                                   None, MLA_SCALE, hb=4, tq=256, tk=256, name="mla_prompt")
            o_p = jnp.swapaxes(o_p, 1, 2)[:, :t].reshape(n_p, C_HEADS * C_VD)
            outs["pc"].append(c_p); outs["pr"].append(kr_p)
            qn_s = q_nope[n_p:n_tok]
            qr_s = _rope(q_rope_n[n_p:n_tok].reshape(ns, 1, C_HEADS, C_ROPE), pos_s)[:, 0]
            c_s = c[n_p:n_tok]
            kr_s = _rope(kr_n[n_p:n_tok].reshape(ns, 1, 1, C_ROPE), pos_s)[:, 0, 0]
            first_row = lambda a: jnp.pad(a[:, None, :], ((0, 0), (0, LANES - 1), (0, 0)))
            o_s = _mla_sample(qn_s, qr_s.astype(BF16), c_k_nope_norm[li], first_row(c_s), first_row(kr_s),
                              w_uk.T.astype(BF16), w_uv.astype(BF16), cache_mla_latent, cache_mla_rope, page_table, li)
            outs["sc"].append(c_s.reshape(ns, 1, C_KV_RANK)); outs["sr"].append(kr_s.reshape(ns, 1, C_ROPE))
            o_all = jnp.concatenate([o_p, o_s.reshape(ns, -1), jnp.zeros((m_rows - n_tok, o_p.shape[-1]), F32)], axis=0)
            x = _matmul(o_all, c_w_out[li].astype(BF16), residual=x, name="mla_out_proj")
        x = _hier_moe(x, n_tok, norm_ffn[layer], moe_w_group[layer], moe_b_group[layer], moe_w_expert[layer],
                      moe_b_expert[layer], moe_w_gate[layer], moe_w_up[layer], moe_w_down[layer])

    y_prompt = x[:n_p].reshape(b, t, d)[:, N_META:]
    y_sample = x[n_p:n_tok].reshape(ns, ds, d)
    st = lambda name: jnp.stack(outs[name])
    return (y_prompt, y_sample, st("pS"), st("pconv"), st("pk"), st("pv"), st("pf"), st("pc"), st("pr"),
            st("sS"), st("sconv"), st("sk"), st("sv"), st("sf"), st("sc"), st("sr"))
```

```python
import functools
import math

import numpy as np
import jax
import jax.numpy as jnp
from jax import lax
from jax.experimental import pallas as pl
from jax.experimental.pallas import tpu as pltpu

F32 = jnp.float32
BF16 = jnp.bfloat16
HI = lax.Precision.HIGHEST

EPS = 1e-6
NEG_INF = -1e30
N_META = 16
LANES = 128
VMEM_LIMIT = 56 * 1024 * 1024

A_HEADS = 8
A_DK = 128
A_DV = 128
A_CONV = 4
A_CHUNK = 64
A_QK = A_HEADS * A_DK
A_V = A_HEADS * A_DV
A_CONV_CH = 2 * A_QK + A_V
B_HEADS = 8
B_KV_HEADS = 2
B_HD = 128
B_GROUP = B_HEADS // B_KV_HEADS
FOX_SCALE = B_HD ** -0.5
C_HEADS = 16
C_Q_RANK = 512
C_KV_RANK = 512
C_NOPE = 128
C_ROPE = 64
C_VD = 128
ROPE_THETA = 10000.0
MLA_SCALE = (C_NOPE + C_ROPE) ** -0.5
N_GROUPS = 8
EXPERTS_PER_GROUP = 8
N_EXPERTS = N_GROUPS * EXPERTS_PER_GROUP
TOP_K = 2
MOE_BLOCK = 128
FOX_PPS = 16
MLA_PPS = 16

NT_DIMS = (((1,), (1,)), ((), ()))


def _cparams(*sem):
    return pltpu.CompilerParams(dimension_semantics=sem, vmem_limit_bytes=VMEM_LIMIT)


def _round_up(n, m):
    return (n + m - 1) // m * m


def _pick_tile(n, candidates):
    for c in candidates:
        if n % c == 0:
            return c
    return n


def _bdot(a, b):
    return jnp.dot(a.astype(BF16), b.astype(BF16), preferred_element_type=F32)


def _bdot_nt(a, b):
    return lax.dot_general(a.astype(BF16), b.astype(BF16), NT_DIMS, preferred_element_type=F32)


def _hdot(a, b):
    return jnp.dot(a, b, preferred_element_type=F32, precision=HI)


def _hdot_nt(a, b):
    return lax.dot_general(a, b, NT_DIMS, preferred_element_type=F32, precision=HI)


def _split(a):
    hi = a.astype(BF16)
    return hi, (a - hi.astype(F32)).astype(BF16)


def _x3dot(a, b):
    a_hi, a_lo = _split(a)
    b_hi, b_lo = _split(b)
    d = lambda x, y: jnp.dot(x, y, preferred_element_type=F32)
    return d(a_hi, b_hi) + (d(a_hi, b_lo) + d(a_lo, b_hi))


def _mm_kernel(*refs, norm, has_res):
    it = iter(refs)
    x_ref = next(it)
    g_ref = next(it) if norm else None
    w_ref = next(it)
    r_ref = next(it) if has_res else None
    o_ref = next(it)
    xn_ref = next(it)

    @pl.when(pl.program_id(1) == 0)
    def _():
        x = x_ref[...].astype(F32)
        if norm:
            x = x * lax.rsqrt(jnp.mean(x * x, axis=-1, keepdims=True) + EPS) * g_ref[...]
        xn_ref[...] = x.astype(BF16)

    acc = jnp.dot(xn_ref[...], w_ref[...], preferred_element_type=F32)
    if has_res:
        acc = acc + r_ref[...]
    o_ref[...] = acc.astype(o_ref.dtype)


def _matmul(x, w, gain=None, residual=None, out_dtype=F32, name="matmul"):
    m, k = x.shape
    n = w.shape[1]
    tm = _pick_tile(m, (768, 512, 256, 128))
    tn = _pick_tile(n, (1024, 768, 512, 384, 256, 128))
    norm = gain is not None
    has_res = residual is not None
    args = [x]
    in_specs = [pl.BlockSpec((tm, k), lambda i, j: (i, 0))]
    if norm:
        args.append(gain.reshape(1, k).astype(F32))
        in_specs.append(pl.BlockSpec((1, k), lambda i, j: (0, 0)))
    args.append(w)
    in_specs.append(pl.BlockSpec((k, tn), lambda i, j: (0, j)))
    if has_res:
        args.append(residual)
        in_specs.append(pl.BlockSpec((tm, tn), lambda i, j: (i, j)))
    return pl.pallas_call(
        functools.partial(_mm_kernel, norm=norm, has_res=has_res),
        out_shape=jax.ShapeDtypeStruct((m, n), out_dtype),
        grid=(m // tm, n // tn),
        in_specs=in_specs,
        out_specs=pl.BlockSpec((tm, tn), lambda i, j: (i, j)),
        scratch_shapes=[pltpu.VMEM((tm, k), BF16)],
        compiler_params=_cparams("parallel", "arbitrary"),
        name=name,
    )(*args)


def _lane_tile(a, n):
    return a if n == 1 else jnp.concatenate([a] * n, axis=-1)


def _flash_kernel(*refs, scale, tq, tk, hb, shared_kv, has_bias):
    it = iter(refs)
    q_ref, k_ref, v_ref = next(it), next(it), next(it)
    b_ref = next(it) if has_bias else None
    o_ref, m_ref, l_ref, acc_ref = next(it), next(it), next(it), next(it)
    qi = pl.program_id(2)
    dv_tiles = acc_ref.shape[-1] // LANES

    m_ref[...] = jnp.full(m_ref.shape, NEG_INF, F32)
    l_ref[...] = jnp.zeros(l_ref.shape, F32)
    acc_ref[...] = jnp.zeros(acc_ref.shape, F32)

    def block(k0, visible):
        for h in range(hb):
            kh = 0 if shared_kv else h
            kblk = k_ref[kh, pl.ds(k0, tk), :]
            vblk = v_ref[kh, pl.ds(k0, tk), :]
            s = lax.dot_general(q_ref[h], kblk, NT_DIMS, preferred_element_type=F32) * scale
            if has_bias:
                s = s + b_ref[h, :, pl.ds(k0, tk)]
            if visible is not None:
                s = jnp.where(visible, s, NEG_INF)
            m_prev = m_ref[h]
            m_new = jnp.maximum(m_prev, jnp.max(s, axis=-1, keepdims=True))
            p = jnp.exp(s - _lane_tile(m_new, tk // LANES))
            alpha = jnp.exp(m_prev - m_new)
            l_ref[h] = alpha * l_ref[h] + jnp.sum(p, axis=-1, keepdims=True)
            acc_ref[h] = _lane_tile(alpha, dv_tiles) * acc_ref[h] + jnp.dot(
                p.astype(BF16), vblk, preferred_element_type=F32)
            m_ref[h] = m_new

    def full_block(ki, carry):
        block(pl.multiple_of(ki * tk, tk), None)
        return carry

    lax.fori_loop(0, qi * (tq // tk), full_block, 0)
    row = lax.broadcasted_iota(jnp.int32, (tq, tk), 0)
    col = lax.broadcasted_iota(jnp.int32, (tq, tk), 1)
    for d in range(tq // tk):
        block(pl.multiple_of(qi * tq + d * tk, tk), col + d * tk <= row)
    o_ref[...] = (acc_ref[...] / _lane_tile(l_ref[...], dv_tiles)).astype(o_ref.dtype)


def _flash_attention(q, k, v, kbias, scale, hb, tq, tk, name):
    b, h, tp, dq = q.shape
    hk = k.shape[1]
    dv = v.shape[-1]
    shared_kv = hk != h
    if shared_kv:
        assert hk * hb == h
    assert tq % tk == 0 and tp % tq == 0
    kvb = 1 if shared_kv else hb
    grp = lambda bi, hg, qi: (bi, hg, 0, 0)
    args = [q, k, v]
    in_specs = [
        pl.BlockSpec((None, hb, tq, dq), lambda bi, hg, qi: (bi, hg, qi, 0)),
        pl.BlockSpec((None, kvb, tp, dq), grp),
        pl.BlockSpec((None, kvb, tp, dv), grp),
    ]
    has_bias = kbias is not None
    if has_bias:
        args.append(kbias)
        in_specs.append(pl.BlockSpec((None, hb, 1, tp), grp))
    return pl.pallas_call(
        functools.partial(_flash_kernel, scale=scale, tq=tq, tk=tk, hb=hb, shared_kv=shared_kv, has_bias=has_bias),
        out_shape=jax.ShapeDtypeStruct((b, h, tp, dv), F32),
        grid=(b, h // hb, tp // tq),
        in_specs=in_specs,
        out_specs=pl.BlockSpec((None, hb, tq, dv), lambda bi, hg, qi: (bi, hg, qi, 0)),
        scratch_shapes=[pltpu.VMEM((hb, tq, LANES), F32), pltpu.VMEM((hb, tq, LANES), F32),
                        pltpu.VMEM((hb, tq, dv), F32)],
        compiler_params=_cparams("parallel", "parallel", "arbitrary"),
        name=name,
    )(*args)


def _gdn_chunk_kernel(q_ref, k_ref, v_ref, gt_ref, o_ref, sout_ref, s_ref, *, chunk, heads):
    c = pl.program_id(1)

    @pl.when(c == 0)
    def _():
        s_ref[...] = jnp.zeros(s_ref.shape, F32)

    hs = range(heads)
    gates = gt_ref[...]
    r = lax.broadcasted_iota(jnp.int32, (chunk, chunk), 0)
    cidx = lax.broadcasted_iota(jnp.int32, (chunk, chunk), 1)
    incl = r >= cidx
    strict = r > cidx
    eye_c = (r == cidx).astype(F32)
    gcum = _hdot(incl.astype(F32), gates)
    eye = (lax.broadcasted_iota(jnp.int32, (LANES, LANES), 0)
           == lax.broadcasted_iota(jnp.int32, (LANES, LANES), 1)).astype(F32)
    gcum_t = _hdot_nt(eye, gcum)
    sl = [slice(h * A_DK, (h + 1) * A_DK) for h in hs]
    q = [q_ref[:, sl[h]] for h in hs]
    k = [k_ref[:, sl[h]] for h in hs]
    v = [v_ref[:, sl[h]] for h in hs]
    s_old = [s_ref[h] for h in hs]
    g_col = [gcum[:, h:h + 1] for h in hs]
    g_last = [gcum[chunk - 1:chunk, h:h + 1] for h in hs]
    beta = [gates[:, heads + h:heads + h + 1] for h in hs]
    decay = [jnp.where(incl, jnp.exp(jnp.where(incl, g_col[h] - gcum_t[h:h + 1, :], 0.0)), 0.0) for h in hs]
    e_g = [jnp.exp(g_col[h]) for h in hs]
    kbeta = [k[h] * beta[h] for h in hs]
    a = [jnp.where(strict, _bdot_nt(kbeta[h], k[h]) * decay[h], 0.0) for h in hs]
    x = [eye_c - a[h] for h in hs]
    p = [_x3dot(a[h], a[h]) for h in hs]
    covered = 2
    while covered < chunk:
        x = [x[h] + _x3dot(x[h], p[h]) for h in hs]
        covered *= 2
        if covered < chunk:
            p = [_x3dot(p[h], p[h]) for h in hs]
    wu = [_x3dot(x[h], jnp.concatenate([kbeta[h] * e_g[h], v[h] * beta[h]], axis=1)) for h in hs]
    att = [_bdot_nt(q[h], k[h]) * decay[h] for h in hs]
    v_new = [wu[h][:, A_DK:] - _bdot(wu[h][:, :A_DK], s_old[h]) for h in hs]
    o = [_bdot(q[h] * e_g[h], s_old[h]) + _bdot(att[h], v_new[h]) for h in hs]
    kg = [k[h] * jnp.exp(g_last[h] - g_col[h]) for h in hs]
    s_new = [s_old[h] * jnp.exp(g_last[h]) + lax.dot_general(
        kg[h].astype(BF16), v_new[h].astype(BF16), (((0,), (0,)), ((), ())), preferred_element_type=F32) for h in hs]
    for h in hs:
        o_ref[:, sl[h]] = o[h]
        s_ref[h] = s_new[h]

    @pl.when(c == pl.num_programs(1) - 1)
    def _():
        sout_ref[...] = s_ref[...]


def _gdn_chunked(q, k, v, gates):
    b, tp, _ = q.shape
    nc = tp // A_CHUNK
    tok_spec = pl.BlockSpec((None, A_CHUNK, A_QK), lambda bi, ci: (bi, ci, 0))
    return pl.pallas_call(
        functools.partial(_gdn_chunk_kernel, chunk=A_CHUNK, heads=A_HEADS),
        out_shape=(jax.ShapeDtypeStruct((b, tp, A_V), F32),
                   jax.ShapeDtypeStruct((b, A_HEADS, A_DK, A_DV), F32)),
        grid=(b, nc),
        in_specs=[tok_spec, tok_spec, tok_spec,
                  pl.BlockSpec((None, A_CHUNK, LANES), lambda bi, ci: (bi, ci, 0))],
        out_specs=(pl.BlockSpec((None, A_CHUNK, A_V), lambda bi, ci: (bi, ci, 0)),
                   pl.BlockSpec((None, A_HEADS, A_DK, A_DV), lambda bi, ci: (bi, 0, 0, 0))),
        scratch_shapes=[pltpu.VMEM((A_HEADS, A_DK, A_DV), F32)],
        compiler_params=_cparams("parallel", "arbitrary"),
        name="gdn_chunked",
    )(q, k, v, gates)


def _gdn_step_kernel(s_ref, q_ref, k_ref, v_ref, gt_ref, o_ref, sout_ref, *, sb, heads):
    eye = (lax.broadcasted_iota(jnp.int32, (LANES, LANES), 0)
           == lax.broadcasted_iota(jnp.int32, (LANES, LANES), 1)).astype(F32)
    zpad = jnp.zeros((LANES - heads, A_DK), F32)
    for i in range(sb):
        k8 = k_ref[i]
        q8 = q_ref[i]
        v8 = v_ref[i]
        k_t = _hdot_nt(eye, jnp.concatenate([k8, zpad], axis=0))
        q_t = _hdot_nt(eye, jnp.concatenate([q8, zpad], axis=0))
        outs = []
        for h in range(heads):
            decay = jnp.exp(gt_ref[i:i + 1, h:h + 1])
            beta = gt_ref[i:i + 1, heads + h:heads + h + 1]
            k_col = k_t[:, h:h + 1]
            s = s_ref[i, h] * decay
            pred = jnp.sum(k_col * s, axis=0, keepdims=True)
            delta = (v8[h:h + 1, :] - pred) * beta
            s = s + k_col * delta
            sout_ref[i, h] = s
            outs.append(jnp.sum(q_t[:, h:h + 1] * s, axis=0, keepdims=True))
        o_ref[i] = jnp.concatenate(outs, axis=0)


def _gdn_step(state, q, k, v, gates):
    n = state.shape[0]
    sb = 8
    vec_spec = pl.BlockSpec((sb, A_HEADS, A_DK), lambda i: (i, 0, 0))
    st_spec = pl.BlockSpec((sb, A_HEADS, A_DK, A_DV), lambda i: (i, 0, 0, 0))
    return pl.pallas_call(
        functools.partial(_gdn_step_kernel, sb=sb, heads=A_HEADS),
        out_shape=(jax.ShapeDtypeStruct((n, A_HEADS, A_DV), F32),
                   jax.ShapeDtypeStruct(state.shape, F32)),
        grid=(n // sb,),
        in_specs=[st_spec, vec_spec, vec_spec, vec_spec, pl.BlockSpec((sb, LANES), lambda i: (i, 0))],
        out_specs=(vec_spec, st_spec),
        compiler_params=_cparams("parallel"),
        name="gdn_step",
    )(state, q, k, v, gates)


def _fox_sample_kernel(pt_ref, q_ref, knew_ref, vnew_ref, lfnew_ref, *refs, pps, scale):
    k_refs = refs[:pps]
    v_refs = refs[pps:2 * pps]
    lf_refs = refs[2 * pps:3 * pps]
    o_ref, m_ref, l_ref, acc_ref, carry_ref = refs[3 * pps:]
    j = pl.program_id(1)
    q = q_ref[...]
    rows = B_KV_HEADS * LANES

    @pl.when(j == 0)
    def _():
        m_ref[...] = jnp.sum(q * knew_ref[...], axis=-1, keepdims=True) * scale
        l_ref[...] = jnp.ones(l_ref.shape, F32)
        acc_ref[...] = vnew_ref[...]
        carry_ref[...] = lfnew_ref[...]

    later2 = (lax.broadcasted_iota(jnp.int32, (LANES, rows), 0)
              > lax.broadcasted_iota(jnp.int32, (LANES, rows), 1) // B_KV_HEADS).astype(F32)
    lf_all = jnp.concatenate([lf_refs[p][...] for p in range(pps)], axis=0)
    within = _hdot(lf_all, later2)
    totals = jnp.sum(lf_all, axis=-1, keepdims=True)
    qb = q.astype(BF16)
    carry = carry_ref[...]
    s_parts = []
    for p in range(pps):
        sl = slice(p * B_HEADS, (p + 1) * B_HEADS)
        s_parts.append(_bdot_nt(qb, k_refs[p][...]) * scale + within[sl] + carry)
        carry = carry + totals[sl]
    carry_ref[...] = carry
    s = jnp.concatenate(s_parts, axis=1)
    own_kv = (lax.broadcasted_iota(jnp.int32, s.shape, 1) % B_KV_HEADS
              == lax.broadcasted_iota(jnp.int32, s.shape, 0) // B_GROUP)
    s = jnp.where(own_kv, s, NEG_INF)
    m_prev = m_ref[...]
    m_new = jnp.maximum(m_prev, jnp.max(s, axis=-1, keepdims=True))
    pr = jnp.exp(s - m_new)
    alpha = jnp.exp(m_prev - m_new)
    l_ref[...] = alpha * l_ref[...] + jnp.sum(pr, axis=-1, keepdims=True)
    v_all = jnp.concatenate([v_refs[p][...].astype(BF16) for p in range(pps)], axis=0)
    acc_ref[...] = alpha * acc_ref[...] + jnp.dot(pr.astype(BF16), v_all, preferred_element_type=F32)
    m_ref[...] = m_new

    @pl.when(j == pl.num_programs(1) - 1)
    def _():
        o_ref[...] = acc_ref[...] / l_ref[...]


def _fox_sample(q, k_new, v_new, lf_new, cache_k, cache_v, cache_lf_t, page_table, li):
    n, n_pages = page_table.shape
    rows = cache_k.shape[2]
    assert rows == B_KV_HEADS * LANES and cache_lf_t.shape[-1] == LANES
    pps = FOX_PPS
    assert n_pages % pps == 0
    steps = n_pages // pps
    pt_flat = page_table.reshape(-1)

    def page_of(bi, j, pt, p):
        return pt[bi * n_pages + (n_pages - 1 - (j * pps + p))]

    head_spec = pl.BlockSpec((None, B_HEADS, B_HD), lambda bi, j, pt: (bi, 0, 0))
    in_specs = [head_spec, head_spec, head_spec,
                pl.BlockSpec((None, B_HEADS, 1), lambda bi, j, pt: (bi, 0, 0))]
    args = [q, k_new, v_new, lf_new]
    for cache in (cache_k, cache_v):
        for p in range(pps):
            in_specs.append(pl.BlockSpec((None, None, rows, B_HD),
                                         functools.partial(lambda bi, j, pt, p: (li, page_of(bi, j, pt, p), 0, 0), p=p)))
            args.append(cache)
    for p in range(pps):
        in_specs.append(pl.BlockSpec((None, None, B_HEADS, LANES),
                                     functools.partial(lambda bi, j, pt, p: (li, page_of(bi, j, pt, p), 0, 0), p=p)))
        args.append(cache_lf_t)
    grid_spec = pltpu.PrefetchScalarGridSpec(
        num_scalar_prefetch=1,
        grid=(n, steps),
        in_specs=in_specs,
        out_specs=pl.BlockSpec((None, B_HEADS, B_HD), lambda bi, j, pt: (bi, 0, 0)),
        scratch_shapes=[pltpu.VMEM((B_HEADS, 1), F32), pltpu.VMEM((B_HEADS, 1), F32),
                        pltpu.VMEM((B_HEADS, B_HD), F32), pltpu.VMEM((B_HEADS, 1), F32)],
    )
    return pl.pallas_call(
        functools.partial(_fox_sample_kernel, pps=pps, scale=FOX_SCALE),
        out_shape=jax.ShapeDtypeStruct((n, B_HEADS, B_HD), F32),
        grid_spec=grid_spec,
        compiler_params=_cparams("parallel", "arbitrary"),
        name="fox_sample",
    )(pt_flat, *args)


def _mla_sample_kernel(pt_ref, qn_ref, qr_ref, gain_ref, cnew_ref, rnew_ref, wukt_ref, wuv_ref, *refs, pps, scale):
    lat_refs = refs[:pps]
    rope_refs = refs[pps:2 * pps]
    o_ref, wq_ref, m_ref, l_ref, acc_ref = refs[2 * pps:]
    bi = pl.program_id(0)
    j = pl.program_id(1)
    nrow = C_HEADS * C_NOPE
    head_of_col = lax.broadcasted_iota(jnp.int32, (C_HEADS, nrow), 1) // C_NOPE
    own = head_of_col == lax.broadcasted_iota(jnp.int32, (C_HEADS, nrow), 0)

    @pl.when((bi == 0) & (j == 0))
    def _():
        wq_ref[0:nrow, :] = wukt_ref[...]

    def attend(cc, rr_t, valid):
        proj = lax.dot_general(wq_ref[...], cc, NT_DIMS, preferred_element_type=F32)
        kp = proj[0:nrow].reshape(C_HEADS, C_NOPE, cc.shape[0])
        rs = lax.rsqrt(jnp.sum(kp * kp, axis=1) * (1.0 / C_NOPE) + EPS)
        s = (proj[nrow:nrow + C_HEADS] * rs + _bdot(qr_ref[...], rr_t)) * scale
        if valid is not None:
            s = jnp.where(valid, s, NEG_INF)
        m_prev = m_ref[...]
        m_new = jnp.maximum(m_prev, jnp.max(s, axis=-1, keepdims=True))
        p = jnp.exp(s - m_new)
        alpha = jnp.exp(m_prev - m_new)
        l_ref[...] = alpha * l_ref[...] + jnp.sum(p, axis=-1, keepdims=True)
        acc_ref[...] = alpha * acc_ref[...] + _bdot(p, cc)
        m_ref[...] = m_new

    @pl.when(j == 0)
    def _():
        qg = jnp.concatenate([qn_ref[...]] * C_HEADS, axis=1) * gain_ref[...]
        qbd = jnp.where(own, qg, 0.0)
        wq_ref[nrow:nrow + C_HEADS, :] = _bdot(qbd, wukt_ref[...]).astype(BF16)
        m_ref[...] = jnp.full(m_ref.shape, NEG_INF, F32)
        l_ref[...] = jnp.zeros(l_ref.shape, F32)
        acc_ref[...] = jnp.zeros(acc_ref.shape, F32)
        first = lax.broadcasted_iota(jnp.int32, (1, LANES), 1) == 0
        attend(cnew_ref[...].astype(BF16), rnew_ref[...].astype(BF16), first)

    cc = jnp.concatenate([lat_refs[p][...] for p in range(pps)], axis=0).astype(BF16)
    rr_t = jnp.concatenate([rope_refs[p][...] for p in range(pps)], axis=1).astype(BF16)
    attend(cc, rr_t, None)

    @pl.when(j == pl.num_programs(1) - 1)
    def _():
        o_lat = acc_ref[...] / l_ref[...]
        full = _bdot(o_lat, wuv_ref[...])
        o_ref[...] = jnp.sum(jnp.where(own, full, 0.0), axis=0, keepdims=True)


def _mla_sample(q_nope, q_rope, k_gain, c_new, r_new, w_uk_t, w_uv, cache_lat, cache_rope_t, page_table, li):
    n, n_pages = page_table.shape
    page = cache_lat.shape[2]
    assert page == LANES
    pps = MLA_PPS
    assert n_pages % pps == 0
    steps = n_pages // pps
    pt_flat = page_table.reshape(-1)
    nrow = C_HEADS * C_NOPE
    gain_t = jnp.tile(k_gain.astype(F32), C_HEADS).reshape(1, nrow)

    def page_of(bi, j, pt, p):
        return pt[bi * n_pages + j * pps + p]

    in_specs = [
        pl.BlockSpec((None, C_HEADS, C_NOPE), lambda bi, j, pt: (bi, 0, 0)),
        pl.BlockSpec((None, C_HEADS, C_ROPE), lambda bi, j, pt: (bi, 0, 0)),
        pl.BlockSpec((1, nrow), lambda bi, j, pt: (0, 0)),
        pl.BlockSpec((None, page, C_KV_RANK), lambda bi, j, pt: (bi, 0, 0)),
        pl.BlockSpec((None, C_ROPE, page), lambda bi, j, pt: (bi, 0, 0)),
        pl.BlockSpec((nrow, C_KV_RANK), lambda bi, j, pt: (0, 0)),
        pl.BlockSpec((C_KV_RANK, C_HEADS * C_VD), lambda bi, j, pt: (0, 0)),
    ]
    args = [q_nope, q_rope, gain_t, c_new, r_new, w_uk_t, w_uv]
    for cache, blk in ((cache_lat, (page, C_KV_RANK)), (cache_rope_t, (C_ROPE, page))):
        for p in range(pps):
            in_specs.append(pl.BlockSpec((None, None) + blk,
                                         functools.partial(lambda bi, j, pt, p: (li, page_of(bi, j, pt, p), 0, 0), p=p)))
            args.append(cache)
    grid_spec = pltpu.PrefetchScalarGridSpec(
        num_scalar_prefetch=1,
        grid=(n, steps),
        in_specs=in_specs,
        out_specs=pl.BlockSpec((None, 1, C_HEADS * C_VD), lambda bi, j, pt: (bi, 0, 0)),
        scratch_shapes=[pltpu.VMEM((nrow + C_HEADS, C_KV_RANK), BF16),
                        pltpu.VMEM((C_HEADS, 1), F32), pltpu.VMEM((C_HEADS, 1), F32),
                        pltpu.VMEM((C_HEADS, C_KV_RANK), F32)],
    )
    return pl.pallas_call(
        functools.partial(_mla_sample_kernel, pps=pps, scale=MLA_SCALE),
        out_shape=jax.ShapeDtypeStruct((n, 1, C_HEADS * C_VD), F32),
        grid_spec=grid_spec,
        compiler_params=_cparams("arbitrary", "arbitrary"),
        name="mla_sample",
    )(pt_flat, *args)


def _router_kernel(x_ref, g_ref, w_ref, b_ref, h_ref, eid_ref, gate_ref):
    x = x_ref[...]
    xn = x * lax.rsqrt(jnp.mean(x * x, axis=-1, keepdims=True) + EPS) * g_ref[...]
    h_ref[...] = xn
    lg = _hdot(xn, w_ref[...]) + b_ref[...]
    lane = lax.broadcasted_iota(jnp.int32, lg.shape, 1)
    lane_f = lane.astype(F32)
    ninf = -jnp.inf
    is_g = lane < N_GROUPS
    gmax = jnp.max(jnp.where(is_g, lg, ninf), axis=-1, keepdims=True)
    gi = jnp.min(jnp.where(is_g & (lg == gmax), lane_f, float(LANES)), axis=-1, keepdims=True)
    gsum = jnp.sum(jnp.exp(jnp.where(is_g, lg - gmax, ninf)), axis=-1, keepdims=True)
    grp_p = 1.0 / gsum
    e_lane = lane - N_GROUPS
    group_of = lax.shift_right_arithmetic(e_lane, int(math.log2(EXPERTS_PER_GROUP)))
    sel = (e_lane >= 0) & (e_lane < N_EXPERTS) & (group_of.astype(F32) == gi)
    m1 = jnp.max(jnp.where(sel, lg, ninf), axis=-1, keepdims=True)
    i1 = jnp.min(jnp.where(sel & (lg == m1), lane_f, 2.0 * LANES), axis=-1, keepdims=True)
    sel2 = sel & (lane_f != i1)
    m2 = jnp.max(jnp.where(sel2, lg, ninf), axis=-1, keepdims=True)
    i2 = jnp.min(jnp.where(sel2 & (lg == m2), lane_f, 2.0 * LANES), axis=-1, keepdims=True)
    r = jnp.exp(m2 - m1)
    g1 = grp_p / (1.0 + r)
    g2 = g1 * r
    eid_ref[...] = jnp.where(lane == 0, i1, jnp.where(lane == 1, i2, float(N_GROUPS))).astype(jnp.int32) - N_GROUPS
    gate_ref[...] = jnp.where(lane == 0, g1, jnp.where(lane == 1, g2, 0.0))


def _router(x, gain, w_router, b_router):
    m, d = x.shape
    tm = _pick_tile(m, (256, 128))
    row = lambda i: (i, 0)
    fixed = lambda i: (0, 0)
    return pl.pallas_call(
        _router_kernel,
        out_shape=(jax.ShapeDtypeStruct((m, d), F32), jax.ShapeDtypeStruct((m, LANES), jnp.int32),
                   jax.ShapeDtypeStruct((m, LANES), F32)),
        grid=(m // tm,),
        in_specs=[pl.BlockSpec((tm, d), row), pl.BlockSpec((1, d), fixed), pl.BlockSpec((d, LANES), fixed),
                  pl.BlockSpec((1, LANES), fixed)],
        out_specs=(pl.BlockSpec((tm, d), row), pl.BlockSpec((tm, LANES), row), pl.BlockSpec((tm, LANES), row)),
        compiler_params=_cparams("parallel"),
        name="moe_router",
    )(x, gain.reshape(1, d).astype(F32), w_router, b_router)


def _expert_kernel(be_ref, na_ref, src_ref, nsrc_ref, dst_ref, h_hbm, wg_ref, wu_ref, wd_ref, out_hbm,
                   xbuf, ybuf, gsem, ssem):
    i = pl.program_id(0)
    na = na_ref[0]
    slot = i % 2

    def gather_copies(idx_ref, s):
        return [pltpu.make_async_copy(h_hbm.at[pl.ds(idx_ref[0, r], 1)], xbuf.at[s, pl.ds(r, 1)], gsem.at[s])
                for r in range(MOE_BLOCK)]

    def scatter_copies(s, use_dst):
        return [pltpu.make_async_copy(ybuf.at[s, pl.ds(r, 1)],
                                      out_hbm.at[pl.ds(dst_ref[0, r] if use_dst else 0, 1)], ssem.at[s])
                for r in range(MOE_BLOCK)]

    @pl.when(i < na)
    def _():
        @pl.when(i == 0)
        def _():
            for c in gather_copies(src_ref, slot):
                c.start()

        for c in gather_copies(src_ref, slot):
            c.wait()

        @pl.when(i + 1 < na)
        def _():
            for c in gather_copies(nsrc_ref, 1 - slot):
                c.start()

        @pl.when(i >= 2)
        def _():
            for c in scatter_copies(slot, False):
                c.wait()

        x = xbuf[slot].astype(BF16)
        gate = jnp.dot(x, wg_ref[...].astype(BF16), preferred_element_type=F32)
        up = jnp.dot(x, wu_ref[...].astype(BF16), preferred_element_type=F32)
        hidden = gate * jax.nn.sigmoid(gate) * up
        ybuf[slot] = jnp.dot(hidden.astype(BF16), wd_ref[...].astype(BF16), preferred_element_type=F32)
        for c in scatter_copies(slot, True):
            c.start()

        @pl.when(i == na - 1)
        def _():
            for c in scatter_copies(slot, False):
                c.wait()

            @pl.when(i >= 1)
            def _():
                for c in scatter_copies(1 - slot, False):
                    c.wait()


def _experts(h, plan, w_gate, w_up, w_down, layer):
    block_expert, n_active, src_row, dst_row = plan
    n_blocks = block_expert.shape[0]
    n_slots = n_blocks * MOE_BLOCK
    d = h.shape[1]
    de = w_gate.shape[-1]
    src3 = src_row.reshape(n_blocks, 1, MOE_BLOCK)
    dst3 = dst_row.reshape(n_blocks, 1, MOE_BLOCK)
    smem_blk = lambda f: pl.BlockSpec((None, 1, MOE_BLOCK), f, memory_space=pltpu.SMEM)
    grid_spec = pltpu.PrefetchScalarGridSpec(
        num_scalar_prefetch=2,
        grid=(n_blocks,),
        in_specs=[smem_blk(lambda i, be, na: (i, 0, 0)),
                  smem_blk(lambda i, be, na: (jnp.minimum(i + 1, n_blocks - 1), 0, 0)),
                  smem_blk(lambda i, be, na: (i, 0, 0)),
                  pl.BlockSpec(memory_space=pl.ANY),
                  pl.BlockSpec((None, None, d, de), lambda i, be, na: (layer, be[i], 0, 0)),
                  pl.BlockSpec((None, None, d, de), lambda i, be, na: (layer, be[i], 0, 0)),
                  pl.BlockSpec((None, None, de, d), lambda i, be, na: (layer, be[i], 0, 0))],
        out_specs=pl.BlockSpec(memory_space=pl.ANY),
        scratch_shapes=[pltpu.VMEM((2, MOE_BLOCK, d), F32), pltpu.VMEM((2, MOE_BLOCK, d), F32),
                        pltpu.SemaphoreType.DMA((2,)), pltpu.SemaphoreType.DMA((2,))],
    )
    return pl.pallas_call(
        _expert_kernel,
        out_shape=jax.ShapeDtypeStruct((n_slots, d), F32),
        grid_spec=grid_spec,
        compiler_params=_cparams("arbitrary"),
        name="moe_experts",
    )(block_expert, n_active, src3, src3, dst3, h, w_gate, w_up, w_down)


def _combine_kernel(x_ref, y_ref, g_ref, o_ref):
    g = g_ref[...]
    d = x_ref.shape[1]
    o_ref[...] = x_ref[...] + g[:, 0:1] * y_ref[:, 0:d] + g[:, 1:2] * y_ref[:, d:2 * d]


def _combine(x, y_pairs, gate):
    m, d = x.shape
    tm = _pick_tile(m, (256, 128))
    y2 = y_pairs.reshape(y_pairs.shape[0] // TOP_K, TOP_K * d)
    row = lambda i: (i, 0)
    return pl.pallas_call(
        _combine_kernel,
        out_shape=jax.ShapeDtypeStruct((m, d), F32),
        grid=(m // tm,),
        in_specs=[pl.BlockSpec((tm, d), row), pl.BlockSpec((tm, TOP_K * d), row), pl.BlockSpec((tm, LANES), row)],
        out_specs=pl.BlockSpec((tm, d), row),
        compiler_params=_cparams("parallel"),
        name="moe_combine",
    )(x, y2, gate)


def _moe_plan(eid):
    n = eid.shape[0]
    nk = n * TOP_K
    flat_e = eid.reshape(-1)
    order = jnp.argsort(flat_e).astype(jnp.int32)
    counts = jnp.sum(flat_e[:, None] == jnp.arange(N_EXPERTS, dtype=jnp.int32)[None, :], axis=0, dtype=jnp.int32)
    padded = (counts + MOE_BLOCK - 1) // MOE_BLOCK * MOE_BLOCK
    pad_end = jnp.cumsum(padded)
    pad_start = pad_end - padded
    seg_start = jnp.cumsum(counts) - counts
    n_slots = _round_up(nk + N_EXPERTS * (MOE_BLOCK - 1), MOE_BLOCK)
    n_blocks = n_slots // MOE_BLOCK
    n_active = (pad_end[-1] // MOE_BLOCK).astype(jnp.int32)
    blk = jnp.arange(n_blocks, dtype=jnp.int32)
    block_expert = jnp.minimum(jnp.searchsorted(pad_end, blk * MOE_BLOCK, side='right'), N_EXPERTS - 1).astype(jnp.int32)
    block_expert = jnp.where(blk < n_active, block_expert, block_expert[n_active - 1])
    slot = jnp.arange(n_slots, dtype=jnp.int32)
    e_s = jnp.repeat(block_expert, MOE_BLOCK)
    pos = slot - pad_start[e_s]
    valid = (pos < counts[e_s]) & (slot < pad_end[-1])
    pair = order[jnp.clip(seg_start[e_s] + pos, 0, nk - 1)]
    src_row = jnp.where(valid, pair // TOP_K, 0).astype(jnp.int32)
    spare = nk + jnp.cumsum((~valid).astype(jnp.int32)) - 1
    dst_row = jnp.where(valid, pair, spare).astype(jnp.int32)
    return block_expert, n_active.reshape(1), src_row, dst_row


def _hier_moe(x, layer, gain, w_group, b_group, w_expert, b_expert, w_gate, w_up, w_down):
    m, d = x.shape
    w_router = jnp.zeros((d, LANES), F32).at[:, :N_GROUPS].set(w_group).at[:, N_GROUPS:N_GROUPS + N_EXPERTS].set(w_expert)
    b_router = jnp.zeros((1, LANES), F32).at[0, :N_GROUPS].set(b_group).at[0, N_GROUPS:N_GROUPS + N_EXPERTS].set(b_expert)
    h, eid, gate = _router(x, gain, w_router, b_router)
    plan = _moe_plan(eid[:, :TOP_K])
    y_pairs = _experts(h, plan, w_gate, w_up, w_down, layer)
    return _combine(x, y_pairs, gate)


def _rms(x, g):
    xf = x.astype(F32)
    return xf * lax.rsqrt(jnp.mean(xf * xf, axis=-1, keepdims=True) + EPS) * g.astype(F32)


def _l2(x):
    return x * lax.rsqrt(jnp.sum(x * x, axis=-1, keepdims=True) + EPS)


def _rope(x, pos):
    half = C_ROPE // 2
    inv = ROPE_THETA ** (-jnp.arange(half, dtype=F32) / half)
    ang = pos.astype(F32)[:, None] * inv[None, :]
    cos = jnp.cos(ang)[None, :, None, :]
    sin = jnp.sin(ang)[None, :, None, :]
    x1, x2 = x[..., :half], x[..., half:]
    return jnp.concatenate([x1 * cos - x2 * sin, x2 * cos + x1 * sin], axis=-1)


def _pack_ab_w_in(w):
    sizes = (A_CONV_CH, A_V, A_HEADS, A_HEADS, B_HEADS * B_HD, B_KV_HEADS * B_HD, B_KV_HEADS * B_HD, B_HEADS)
    qkv, z, a_raw, b_raw, q_b, k_b, v_b, f_raw = jnp.split(w, np.cumsum(sizes)[:-1].tolist(), axis=-1)
    small = jnp.concatenate([a_raw, b_raw, f_raw], axis=-1)
    small = jnp.pad(small, ((0, 0), (0, LANES - small.shape[-1])))
    packed = jnp.concatenate([qkv, z, q_b, k_b, v_b, small], axis=-1)
    n_pad = _round_up(packed.shape[-1], 768)
    return jnp.pad(packed, ((0, 0), (0, n_pad - packed.shape[-1]))).astype(BF16)


def _ab_pre(proj, conv_buf, conv_w, a_log, dt_bias, fgate_bias, q_norm, k_norm):
    b, t, _ = proj.shape
    o = 0
    qkv_a = proj[..., o:o + A_CONV_CH]; o += A_CONV_CH
    z = proj[..., o:o + A_V]; o += A_V
    q_b = proj[..., o:o + B_HEADS * B_HD]; o += B_HEADS * B_HD
    k_b = proj[..., o:o + B_KV_HEADS * B_HD]; o += B_KV_HEADS * B_HD
    v_b = proj[..., o:o + B_KV_HEADS * B_HD]; o += B_KV_HEADS * B_HD
    a_raw = proj[..., o:o + A_HEADS]
    b_raw = proj[..., o + A_HEADS:o + 2 * A_HEADS]
    f_raw = proj[..., o + 2 * A_HEADS:o + 2 * A_HEADS + B_HEADS]
    xp = jnp.concatenate([conv_buf.astype(F32), qkv_a], axis=1)
    conv = xp[:, 0:t] * conv_w[0]
    for i in range(1, A_CONV):
        conv = conv + xp[:, i:i + t] * conv_w[i]
    conv = jax.nn.silu(conv)
    new_buf = xp[:, t:]
    qa = _l2(conv[..., :A_QK].reshape(b, t, A_HEADS, A_DK)) * (A_DK ** -0.5)
    ka = _l2(conv[..., A_QK:2 * A_QK].reshape(b, t, A_HEADS, A_DK))
    va = conv[..., 2 * A_QK:].reshape(b, t, A_HEADS, A_DV)
    beta = jax.nn.sigmoid(b_raw)
    g = -jnp.exp(a_log.astype(F32)) * jax.nn.softplus(a_raw + dt_bias.astype(F32))
    qb = _rms(q_b.reshape(b, t, B_HEADS, B_HD), q_norm)
    kb = _rms(k_b.reshape(b, t, B_KV_HEADS, B_HD), k_norm)
    vb = v_b.reshape(b, t, B_KV_HEADS, B_HD)
    logf = jax.nn.log_sigmoid(f_raw + fgate_bias.astype(F32))
    return (qa, ka, va, g, beta, z, new_buf), (qb, kb, vb, logf)


def _gate_lanes(g, beta):
    both = jnp.concatenate([g, beta], axis=-1)
    return jnp.pad(both, [(0, 0)] * (both.ndim - 1) + [(0, LANES - both.shape[-1])])


def _pad_seq(a, tp):
    return jnp.pad(a, [(0, 0), (0, tp - a.shape[1])] + [(0, 0)] * (a.ndim - 2))


def _heads_first(a, tp):
    return jnp.swapaxes(_pad_seq(a, tp), 1, 2).astype(BF16)


def _gdn_gated_out(o_gdn, z, gdn_norm):
    b, t = o_gdn.shape[:2]
    og = _rms(o_gdn, gdn_norm) * jax.nn.silu(z.reshape(b, t, A_HEADS, A_DV))
    return og.reshape(b, t, A_V)


def kernel(x_prompt, x_sample, state_gdn, state_gdn_conv, cache_fox_k, cache_fox_v, cache_fox_logf, cache_mla_latent, cache_mla_rope, page_table, meta_tokens, norm_mix, norm_ffn, ab_w_in, ab_conv_w, ab_a_log, ab_dt_bias, ab_gdn_norm, ab_fgate_bias, ab_q_norm, ab_k_norm, ab_w_out, c_w_in, c_q_a_norm, c_kv_a_norm, c_w_q_b, c_w_uk, c_w_uv, c_q_nope_norm, c_q_rope_norm, c_k_nope_norm, c_k_rope_norm, c_w_out, moe_w_group, moe_b_group, moe_w_expert, moe_b_expert, moe_w_gate, moe_w_up, moe_w_down):
    b, seq, d = x_prompt.shape
    ns, ds, _ = x_sample.shape
    assert ds == 1
    depth = norm_mix.shape[0]
    t = seq + N_META
    n_p = b * t
    n_tok = n_p + ns
    m_rows = _round_up(n_tok, 256)
    past = page_table.shape[1] * cache_fox_k.shape[2]
    tp_attn = _round_up(t, 256)
    tp_gdn = _round_up(t, A_CHUNK)

    meta = jnp.broadcast_to(meta_tokens.astype(F32)[None], (b, N_META, d))
    xp0 = jnp.concatenate([meta, x_prompt], axis=1).reshape(n_p, d)
    x = jnp.concatenate([xp0, x_sample.reshape(ns, d), jnp.zeros((m_rows - n_tok, d), F32)], axis=0)

    n_layers_ab, pool = cache_fox_k.shape[:2]
    fox_k_rows = cache_fox_k.reshape(n_layers_ab, pool, -1, B_HD)
    fox_v_rows = cache_fox_v.reshape(n_layers_ab, pool, -1, B_HD)
    fox_lf_t = jnp.swapaxes(cache_fox_logf, 2, 3)
    mla_rope_t = jnp.swapaxes(cache_mla_rope, 2, 3)
    pos_p = jnp.arange(t)
    pos_s = past + jnp.arange(ds)
    outs = {name: [] for name in ("pS", "pconv", "pk", "pv", "pf", "pc", "pr", "sS", "sconv", "sk", "sv", "sf", "sc", "sr")}

    for layer in range(depth):
        li = layer // 2
        if layer % 2 == 0:
            proj = _matmul(x, _pack_ab_w_in(ab_w_in[li]), gain=norm_mix[layer], name="ab_in_proj")
            wts = (ab_conv_w[li], ab_a_log[li], ab_dt_bias[li], ab_fgate_bias[li], ab_q_norm[li], ab_k_norm[li])
            (qa, ka, va, g, beta, z, buf), (qb, kb, vb, lf) = _ab_pre(
                proj[:n_p].reshape(b, t, -1), jnp.zeros((b, A_CONV - 1, A_CONV_CH), F32), *wts)
            o_a, s_new = _gdn_chunked(_pad_seq(qa.reshape(b, t, A_QK), tp_gdn), _pad_seq(ka.reshape(b, t, A_QK), tp_gdn),
                                      _pad_seq(va.reshape(b, t, A_V), tp_gdn), _pad_seq(_gate_lanes(g, beta), tp_gdn))
            o_a = o_a[:, :t].reshape(b, t, A_HEADS, A_DV)
            f_cum = jnp.cumsum(lf, axis=1)
            kbias = jnp.swapaxes(_pad_seq(-f_cum, tp_attn), 1, 2)[:, :, None, :]
            o_b = _flash_attention(_heads_first(qb, tp_attn), _heads_first(kb, tp_attn), _heads_first(vb, tp_attn),
                                   kbias, FOX_SCALE, hb=B_GROUP, tq=256, tk=256, name="fox_prompt")
            o_b = jnp.swapaxes(o_b, 1, 2)[:, :t].reshape(b, t, B_HEADS * B_HD)
            mixed_p = jnp.concatenate([_gdn_gated_out(o_a, z, ab_gdn_norm[li]), o_b], axis=-1)
            outs["pS"].append(s_new); outs["pconv"].append(buf)
            outs["pk"].append(kb); outs["pv"].append(vb); outs["pf"].append(lf)
            (qa, ka, va, g, beta, z, buf), (qb, kb, vb, lf) = _ab_pre(
                proj[n_p:n_tok].reshape(ns, 1, -1), state_gdn_conv[li], *wts)
            o_a, s_new = _gdn_step(state_gdn[li], qa[:, 0], ka[:, 0], va[:, 0], _gate_lanes(g[:, 0], beta[:, 0]))
            rep = lambda a: jnp.repeat(a[:, 0], B_GROUP, axis=1)
            o_b = _fox_sample(qb[:, 0], rep(kb), rep(vb), lf[:, 0, :, None], fox_k_rows, fox_v_rows, fox_lf_t,
                              page_table, li)
            mixed_s = jnp.concatenate([_gdn_gated_out(o_a[:, None], z, ab_gdn_norm[li]),
                                       o_b.reshape(ns, 1, B_HEADS * B_HD)], axis=-1)
            outs["sS"].append(s_new); outs["sconv"].append(buf)
            outs["sk"].append(kb); outs["sv"].append(vb); outs["sf"].append(lf)
            mixed = jnp.concatenate([mixed_p.reshape(n_p, -1), mixed_s.reshape(ns, -1),
                                     jnp.zeros((m_rows - n_tok, mixed_p.shape[-1]), F32)], axis=0)
            x = _matmul(mixed, ab_w_out[li].astype(BF16), residual=x, name="ab_out_proj")
        else:
            c_in = c_w_in[li]
            n_in = _round_up(c_in.shape[1], LANES)
            lat = _matmul(x, jnp.pad(c_in, ((0, 0), (0, n_in - c_in.shape[1]))).astype(BF16),
                          gain=norm_mix[layer], name="mla_in_proj")
            q_lat, c, kr = lat[:, :C_Q_RANK], lat[:, C_Q_RANK:C_Q_RANK + C_KV_RANK], lat[:, C_Q_RANK + C_KV_RANK:C_Q_RANK + C_KV_RANK + C_ROPE]
            q = _matmul(q_lat, c_w_q_b[li].reshape(C_Q_RANK, -1).astype(BF16), gain=c_q_a_norm[li], name="mla_q_proj")
            q = q.reshape(m_rows, C_HEADS, C_NOPE + C_ROPE)
            q_nope = _rms(q[..., :C_NOPE], c_q_nope_norm[li])
            q_rope_n = _rms(q[..., C_NOPE:], c_q_rope_norm[li])
            c = _rms(c, c_kv_a_norm[li])
            kr_n = _rms(kr, c_k_rope_norm[li])
            w_uk = c_w_uk[li].reshape(C_KV_RANK, -1)
            w_uv = c_w_uv[li].reshape(C_KV_RANK, -1)
            qn_p = q_nope[:n_p].reshape(b, t, C_HEADS, C_NOPE)
            qr_p = _rope(q_rope_n[:n_p].reshape(b, t, C_HEADS, C_ROPE), pos_p)
            c_p = c[:n_p].reshape(b, t, C_KV_RANK)
            kr_p = _rope(kr_n[:n_p].reshape(b, t, 1, C_ROPE), pos_p)[:, :, 0]
            c_rows = jnp.pad(c[:n_p], ((0, _round_up(n_p, 256) - n_p), (0, 0)))
            kv = _matmul(c_rows, jnp.concatenate([w_uk, w_uv], axis=1).astype(BF16), name="mla_kv_proj")[:n_p]
            k_nope = _rms(kv[:, :C_HEADS * C_NOPE].reshape(b, t, C_HEADS, C_NOPE), c_k_nope_norm[li])
            v_p = kv[:, C_HEADS * C_NOPE:].reshape(b, t, C_HEADS, C_VD)
            zpad = jnp.zeros((b, t, C_HEADS, C_NOPE - C_ROPE), F32)
            q_full = jnp.concatenate([qn_p, qr_p, zpad], axis=-1)
            k_full = jnp.concatenate([k_nope, jnp.broadcast_to(kr_p[:, :, None, :], (b, t, C_HEADS, C_ROPE)), zpad], axis=-1)
            o_p = _flash_attention(_heads_first(q_full, tp_attn), _heads_first(k_full, tp_attn), _heads_first(v_p, tp_attn),
                                   None, MLA_SCALE, hb=4, tq=256, tk=256, name="mla_prompt")
            o_p = jnp.swapaxes(o_p, 1, 2)[:, :t].reshape(n_p, C_HEADS * C_VD)
            outs["pc"].append(c_p); outs["pr"].append(kr_p)
            qn_s = q_nope[n_p:n_tok]
            qr_s = _rope(q_rope_n[n_p:n_tok].reshape(ns, 1, C_HEADS, C_ROPE), pos_s)[:, 0]
            c_s = c[n_p:n_tok]
            kr_s = _rope(kr_n[n_p:n_tok].reshape(ns, 1, 1, C_ROPE), pos_s)[:, 0, 0]
            c_first = jnp.pad(c_s[:, None, :], ((0, 0), (0, LANES - 1), (0, 0)))
            r_first = jnp.pad(kr_s[:, :, None], ((0, 0), (0, 0), (0, LANES - 1)))
            o_s = _mla_sample(qn_s, qr_s.astype(BF16), c_k_nope_norm[li], c_first, r_first,
                              w_uk.T.astype(BF16), w_uv.astype(BF16), cache_mla_latent, mla_rope_t, page_table, li)
            outs["sc"].append(c_s.reshape(ns, 1, C_KV_RANK)); outs["sr"].append(kr_s.reshape(ns, 1, C_ROPE))
            o_all = jnp.concatenate([o_p, o_s.reshape(ns, -1), jnp.zeros((m_rows - n_tok, o_p.shape[-1]), F32)], axis=0)
            x = _matmul(o_all, c_w_out[li].astype(BF16), residual=x, name="mla_out_proj")
        x = _hier_moe(x, layer, norm_ffn[layer], moe_w_group[layer], moe_b_group[layer], moe_w_expert[layer],
                      moe_b_expert[layer], moe_w_gate, moe_w_up, moe_w_down)

    y_prompt = x[:n_p].reshape(b, t, d)[:, N_META:]
    y_sample = x[n_p:n_tok].reshape(ns, ds, d)
    st = lambda name: jnp.stack(outs[name])
    return (y_prompt, y_sample, st("pS"), st("pconv"), st("pk"), st("pv"), st("pf"), st("pc"), st("pr"),
            st("sS"), st("sconv"), st("sk"), st("sv"), st("sf"), st("sc"), st("sr"))
```

```python
import functools
import math

import numpy as np
import jax
import jax.numpy as jnp
from jax import lax
from jax.experimental import pallas as pl
from jax.experimental.pallas import tpu as pltpu

F32 = jnp.float32
BF16 = jnp.bfloat16
HI = lax.Precision.HIGHEST

EPS = 1e-6
NEG_INF = -1e30
N_META = 16
LANES = 128
VMEM_LIMIT = 56 * 1024 * 1024

A_HEADS = 8
A_DK = 128
A_DV = 128
A_CONV = 4
A_CHUNK = 64
A_QK = A_HEADS * A_DK
A_V = A_HEADS * A_DV
A_CONV_CH = 2 * A_QK + A_V
B_HEADS = 8
B_KV_HEADS = 2
B_HD = 128
B_GROUP = B_HEADS // B_KV_HEADS
FOX_SCALE = B_HD ** -0.5
C_HEADS = 16
C_Q_RANK = 512
C_KV_RANK = 512
C_NOPE = 128
C_ROPE = 64
C_VD = 128
ROPE_THETA = 10000.0
MLA_SCALE = (C_NOPE + C_ROPE) ** -0.5
N_GROUPS = 8
EXPERTS_PER_GROUP = 8
N_EXPERTS = N_GROUPS * EXPERTS_PER_GROUP
TOP_K = 2
MOE_BLOCK = 128
FOX_PPS = 16
MLA_PPS = 16

NT_DIMS = (((1,), (1,)), ((), ()))


def _cparams(*sem):
    return pltpu.CompilerParams(dimension_semantics=sem, vmem_limit_bytes=VMEM_LIMIT)


def _round_up(n, m):
    return (n + m - 1) // m * m


def _pick_tile(n, candidates):
    for c in candidates:
        if n % c == 0:
            return c
    return n


def _bdot(a, b):
    return jnp.dot(a.astype(BF16), b.astype(BF16), preferred_element_type=F32)


def _bdot_nt(a, b):
    return lax.dot_general(a.astype(BF16), b.astype(BF16), NT_DIMS, preferred_element_type=F32)


def _hdot(a, b):
    return jnp.dot(a, b, preferred_element_type=F32, precision=HI)


def _hdot_nt(a, b):
    return lax.dot_general(a, b, NT_DIMS, preferred_element_type=F32, precision=HI)


def _split(a):
    hi = a.astype(BF16)
    return hi, (a - hi.astype(F32)).astype(BF16)


def _x3dot(a, b):
    a_hi, a_lo = _split(a)
    b_hi, b_lo = _split(b)
    d = lambda x, y: jnp.dot(x, y, preferred_element_type=F32)
    return d(a_hi, b_hi) + (d(a_hi, b_lo) + d(a_lo, b_hi))


def _mm_kernel(*refs, norm, has_res):
    it = iter(refs)
    x_ref = next(it)
    g_ref = next(it) if norm else None
    w_ref = next(it)
    r_ref = next(it) if has_res else None
    o_ref = next(it)
    xn_ref = next(it)

    @pl.when(pl.program_id(1) == 0)
    def _():
        x = x_ref[...].astype(F32)
        if norm:
            x = x * lax.rsqrt(jnp.mean(x * x, axis=-1, keepdims=True) + EPS) * g_ref[...]
        xn_ref[...] = x.astype(BF16)

    acc = jnp.dot(xn_ref[...], w_ref[...], preferred_element_type=F32)
    if has_res:
        acc = acc + r_ref[...]
    o_ref[...] = acc.astype(o_ref.dtype)


def _matmul(x, w, gain=None, residual=None, out_dtype=F32, name="matmul"):
    m, k = x.shape
    n = w.shape[1]
    tm = _pick_tile(m, (768, 512, 256, 128))
    tn = _pick_tile(n, (1024, 768, 512, 384, 256, 128))
    norm = gain is not None
    has_res = residual is not None
    args = [x]
    in_specs = [pl.BlockSpec((tm, k), lambda i, j: (i, 0))]
    if norm:
        args.append(gain.reshape(1, k).astype(F32))
        in_specs.append(pl.BlockSpec((1, k), lambda i, j: (0, 0)))
    args.append(w)
    in_specs.append(pl.BlockSpec((k, tn), lambda i, j: (0, j)))
    if has_res:
        args.append(residual)
        in_specs.append(pl.BlockSpec((tm, tn), lambda i, j: (i, j)))
    return pl.pallas_call(
        functools.partial(_mm_kernel, norm=norm, has_res=has_res),
        out_shape=jax.ShapeDtypeStruct((m, n), out_dtype),
        grid=(m // tm, n // tn),
        in_specs=in_specs,
        out_specs=pl.BlockSpec((tm, tn), lambda i, j: (i, j)),
        scratch_shapes=[pltpu.VMEM((tm, k), BF16)],
        compiler_params=_cparams("parallel", "arbitrary"),
        name=name,
    )(*args)


def _lane_tile(a, n):
    return a if n == 1 else jnp.concatenate([a] * n, axis=-1)


def _flash_kernel(*refs, scale, tq, tk, hb, shared_kv, has_bias):
    it = iter(refs)
    q_ref, k_ref, v_ref = next(it), next(it), next(it)
    b_ref = next(it) if has_bias else None
    o_ref, m_ref, l_ref, acc_ref = next(it), next(it), next(it), next(it)
    qi = pl.program_id(2)
    dv_tiles = acc_ref.shape[-1] // LANES

    m_ref[...] = jnp.full(m_ref.shape, NEG_INF, F32)
    l_ref[...] = jnp.zeros(l_ref.shape, F32)
    acc_ref[...] = jnp.zeros(acc_ref.shape, F32)

    def block(k0, visible):
        for h in range(hb):
            kh = 0 if shared_kv else h
            kblk = k_ref[kh, pl.ds(k0, tk), :]
            vblk = v_ref[kh, pl.ds(k0, tk), :]
            s = lax.dot_general(q_ref[h], kblk, NT_DIMS, preferred_element_type=F32) * scale
            if has_bias:
                s = s + b_ref[h, :, pl.ds(k0, tk)]
            if visible is not None:
                s = jnp.where(visible, s, NEG_INF)
            m_prev = m_ref[h]
            m_new = jnp.maximum(m_prev, jnp.max(s, axis=-1, keepdims=True))
            p = jnp.exp(s - _lane_tile(m_new, tk // LANES))
            alpha = jnp.exp(m_prev - m_new)
            l_ref[h] = alpha * l_ref[h] + jnp.sum(p, axis=-1, keepdims=True)
            acc_ref[h] = _lane_tile(alpha, dv_tiles) * acc_ref[h] + jnp.dot(
                p.astype(BF16), vblk, preferred_element_type=F32)
            m_ref[h] = m_new

    def full_block(ki, carry):
        block(pl.multiple_of(ki * tk, tk), None)
        return carry

    lax.fori_loop(0, qi * (tq // tk), full_block, 0)
    row = lax.broadcasted_iota(jnp.int32, (tq, tk), 0)
    col = lax.broadcasted_iota(jnp.int32, (tq, tk), 1)
    for d in range(tq // tk):
        block(pl.multiple_of(qi * tq + d * tk, tk), col + d * tk <= row)
    o_ref[...] = (acc_ref[...] / _lane_tile(l_ref[...], dv_tiles)).astype(o_ref.dtype)


def _flash_attention(q, k, v, kbias, scale, hb, tq, tk, name):
    b, h, tp, dq = q.shape
    hk = k.shape[1]
    dv = v.shape[-1]
    shared_kv = hk != h
    if shared_kv:
        assert hk * hb == h
    assert tq % tk == 0 and tp % tq == 0
    kvb = 1 if shared_kv else hb
    grp = lambda bi, hg, qi: (bi, hg, 0, 0)
    args = [q, k, v]
    in_specs = [
        pl.BlockSpec((None, hb, tq, dq), lambda bi, hg, qi: (bi, hg, qi, 0)),
        pl.BlockSpec((None, kvb, tp, dq), grp),
        pl.BlockSpec((None, kvb, tp, dv), grp),
    ]
    has_bias = kbias is not None
    if has_bias:
        args.append(kbias)
        in_specs.append(pl.BlockSpec((None, hb, 1, tp), grp))
    return pl.pallas_call(
        functools.partial(_flash_kernel, scale=scale, tq=tq, tk=tk, hb=hb, shared_kv=shared_kv, has_bias=has_bias),
        out_shape=jax.ShapeDtypeStruct((b, h, tp, dv), F32),
        grid=(b, h // hb, tp // tq),
        in_specs=in_specs,
        out_specs=pl.BlockSpec((None, hb, tq, dv), lambda bi, hg, qi: (bi, hg, qi, 0)),
        scratch_shapes=[pltpu.VMEM((hb, tq, LANES), F32), pltpu.VMEM((hb, tq, LANES), F32),
                        pltpu.VMEM((hb, tq, dv), F32)],
        compiler_params=_cparams("parallel", "parallel", "arbitrary"),
        name=name,
    )(*args)


def _gdn_chunk_kernel(q_ref, k_ref, v_ref, gt_ref, o_ref, sout_ref, s_ref, *, chunk, heads):
    c = pl.program_id(1)

    @pl.when(c == 0)
    def _():
        s_ref[...] = jnp.zeros(s_ref.shape, F32)

    hs = range(heads)
    gates = gt_ref[...]
    r = lax.broadcasted_iota(jnp.int32, (chunk, chunk), 0)
    cidx = lax.broadcasted_iota(jnp.int32, (chunk, chunk), 1)
    incl = r >= cidx
    strict = r > cidx
    eye_c = (r == cidx).astype(F32)
    gcum = _hdot(incl.astype(F32), gates)
    eye = (lax.broadcasted_iota(jnp.int32, (LANES, LANES), 0)
           == lax.broadcasted_iota(jnp.int32, (LANES, LANES), 1)).astype(F32)
    gcum_t = _hdot_nt(eye, gcum)
    sl = [slice(h * A_DK, (h + 1) * A_DK) for h in hs]
    q = [q_ref[:, sl[h]] for h in hs]
    k = [k_ref[:, sl[h]] for h in hs]
    v = [v_ref[:, sl[h]] for h in hs]
    s_old = [s_ref[h] for h in hs]
    g_col = [gcum[:, h:h + 1] for h in hs]
    g_last = [gcum[chunk - 1:chunk, h:h + 1] for h in hs]
    beta = [gates[:, heads + h:heads + h + 1] for h in hs]
    decay = [jnp.where(incl, jnp.exp(jnp.where(incl, g_col[h] - gcum_t[h:h + 1, :], 0.0)), 0.0) for h in hs]
    e_g = [jnp.exp(g_col[h]) for h in hs]
    kbeta = [k[h] * beta[h] for h in hs]
    a = [jnp.where(strict, _bdot_nt(kbeta[h], k[h]) * decay[h], 0.0) for h in hs]
    x = [eye_c - a[h] for h in hs]
    p = [_x3dot(a[h], a[h]) for h in hs]
    covered = 2
    while covered < chunk:
        x = [x[h] + _x3dot(x[h], p[h]) for h in hs]
        covered *= 2
        if covered < chunk:
            p = [_x3dot(p[h], p[h]) for h in hs]
    wu = [_x3dot(x[h], jnp.concatenate([kbeta[h] * e_g[h], v[h] * beta[h]], axis=1)) for h in hs]
    att = [_bdot_nt(q[h], k[h]) * decay[h] for h in hs]
    v_new = [wu[h][:, A_DK:] - _bdot(wu[h][:, :A_DK], s_old[h]) for h in hs]
    o = [_bdot(q[h] * e_g[h], s_old[h]) + _bdot(att[h], v_new[h]) for h in hs]
    kg = [k[h] * jnp.exp(g_last[h] - g_col[h]) for h in hs]
    s_new = [s_old[h] * jnp.exp(g_last[h]) + lax.dot_general(
        kg[h].astype(BF16), v_new[h].astype(BF16), (((0,), (0,)), ((), ())), preferred_element_type=F32) for h in hs]
    for h in hs:
        o_ref[:, sl[h]] = o[h]
        s_ref[h] = s_new[h]

    @pl.when(c == pl.num_programs(1) - 1)
    def _():
        sout_ref[...] = s_ref[...]


def _gdn_chunked(q, k, v, gates):
    b, tp, _ = q.shape
    nc = tp // A_CHUNK
    tok_spec = pl.BlockSpec((None, A_CHUNK, A_QK), lambda bi, ci: (bi, ci, 0))
    return pl.pallas_call(
        functools.partial(_gdn_chunk_kernel, chunk=A_CHUNK, heads=A_HEADS),
        out_shape=(jax.ShapeDtypeStruct((b, tp, A_V), F32),
                   jax.ShapeDtypeStruct((b, A_HEADS, A_DK, A_DV), F32)),
        grid=(b, nc),
        in_specs=[tok_spec, tok_spec, tok_spec,
                  pl.BlockSpec((None, A_CHUNK, LANES), lambda bi, ci: (bi, ci, 0))],
        out_specs=(pl.BlockSpec((None, A_CHUNK, A_V), lambda bi, ci: (bi, ci, 0)),
                   pl.BlockSpec((None, A_HEADS, A_DK, A_DV), lambda bi, ci: (bi, 0, 0, 0))),
        scratch_shapes=[pltpu.VMEM((A_HEADS, A_DK, A_DV), F32)],
        compiler_params=_cparams("parallel", "arbitrary"),
        name="gdn_chunked",
    )(q, k, v, gates)


def _gdn_step_kernel(s_ref, q_ref, k_ref, v_ref, gt_ref, o_ref, sout_ref, *, sb, heads):
    eye = (lax.broadcasted_iota(jnp.int32, (LANES, LANES), 0)
           == lax.broadcasted_iota(jnp.int32, (LANES, LANES), 1)).astype(F32)
    zpad = jnp.zeros((LANES - heads, A_DK), F32)
    for i in range(sb):
        k8 = k_ref[i]
        q8 = q_ref[i]
        v8 = v_ref[i]
        k_t = _hdot_nt(eye, jnp.concatenate([k8, zpad], axis=0))
        q_t = _hdot_nt(eye, jnp.concatenate([q8, zpad], axis=0))
        outs = []
        for h in range(heads):
            decay = jnp.exp(gt_ref[i:i + 1, h:h + 1])
            beta = gt_ref[i:i + 1, heads + h:heads + h + 1]
            k_col = k_t[:, h:h + 1]
            s = s_ref[i, h] * decay
            pred = jnp.sum(k_col * s, axis=0, keepdims=True)
            delta = (v8[h:h + 1, :] - pred) * beta
            s = s + k_col * delta
            sout_ref[i, h] = s
            outs.append(jnp.sum(q_t[:, h:h + 1] * s, axis=0, keepdims=True))
        o_ref[i] = jnp.concatenate(outs, axis=0)


def _gdn_step(state, q, k, v, gates):
    n = state.shape[0]
    sb = 8
    vec_spec = pl.BlockSpec((sb, A_HEADS, A_DK), lambda i: (i, 0, 0))
    st_spec = pl.BlockSpec((sb, A_HEADS, A_DK, A_DV), lambda i: (i, 0, 0, 0))
    return pl.pallas_call(
        functools.partial(_gdn_step_kernel, sb=sb, heads=A_HEADS),
        out_shape=(jax.ShapeDtypeStruct((n, A_HEADS, A_DV), F32),
                   jax.ShapeDtypeStruct(state.shape, F32)),
        grid=(n // sb,),
        in_specs=[st_spec, vec_spec, vec_spec, vec_spec, pl.BlockSpec((sb, LANES), lambda i: (i, 0))],
        out_specs=(vec_spec, st_spec),
        compiler_params=_cparams("parallel"),
        name="gdn_step",
    )(state, q, k, v, gates)


def _fox_sample_kernel(pt_ref, q_ref, knew_ref, vnew_ref, lfnew_ref, *refs, pps, scale):
    k_refs = refs[:pps]
    v_refs = refs[pps:2 * pps]
    lf_refs = refs[2 * pps:3 * pps]
    o_ref, m_ref, l_ref, acc_ref, carry_ref = refs[3 * pps:]
    j = pl.program_id(1)
    q = q_ref[...]
    rows = B_KV_HEADS * LANES

    @pl.when(j == 0)
    def _():
        m_ref[...] = jnp.sum(q * knew_ref[...], axis=-1, keepdims=True) * scale
        l_ref[...] = jnp.ones(l_ref.shape, F32)
        acc_ref[...] = vnew_ref[...]
        carry_ref[...] = lfnew_ref[...]

    later2 = (lax.broadcasted_iota(jnp.int32, (LANES, rows), 0)
              > lax.broadcasted_iota(jnp.int32, (LANES, rows), 1) // B_KV_HEADS).astype(F32)
    lf_all = jnp.concatenate([lf_refs[p][...] for p in range(pps)], axis=0)
    within = _hdot(lf_all, later2)
    totals = jnp.sum(lf_all, axis=-1, keepdims=True)
    qb = q.astype(BF16)
    carry = carry_ref[...]
    s_parts = []
    for p in range(pps):
        sl = slice(p * B_HEADS, (p + 1) * B_HEADS)
        s_parts.append(_bdot_nt(qb, k_refs[p][...]) * scale + within[sl] + carry)
        carry = carry + totals[sl]
    carry_ref[...] = carry
    s = jnp.concatenate(s_parts, axis=1)
    own_kv = (lax.broadcasted_iota(jnp.int32, s.shape, 1) % B_KV_HEADS
              == lax.broadcasted_iota(jnp.int32, s.shape, 0) // B_GROUP)
    s = jnp.where(own_kv, s, NEG_INF)
    m_prev = m_ref[...]
    m_new = jnp.maximum(m_prev, jnp.max(s, axis=-1, keepdims=True))
    pr = jnp.exp(s - m_new)
    alpha = jnp.exp(m_prev - m_new)
    l_ref[...] = alpha * l_ref[...] + jnp.sum(pr, axis=-1, keepdims=True)
    v_all = jnp.concatenate([v_refs[p][...].astype(BF16) for p in range(pps)], axis=0)
    acc_ref[...] = alpha * acc_ref[...] + jnp.dot(pr.astype(BF16), v_all, preferred_element_type=F32)
    m_ref[...] = m_new

    @pl.when(j == pl.num_programs(1) - 1)
    def _():
        o_ref[...] = acc_ref[...] / l_ref[...]


def _fox_sample(q, k_new, v_new, lf_new, cache_k, cache_v, cache_lf_t, page_table, li):
    n, n_pages = page_table.shape
    rows = cache_k.shape[2]
    assert rows == B_KV_HEADS * LANES and cache_lf_t.shape[-1] == LANES
    pps = FOX_PPS
    assert n_pages % pps == 0
    steps = n_pages // pps
    pt_flat = page_table.reshape(-1)

    def page_of(bi, j, pt, p):
        return pt[bi * n_pages + (n_pages - 1 - (j * pps + p))]

    head_spec = pl.BlockSpec((None, B_HEADS, B_HD), lambda bi, j, pt: (bi, 0, 0))
    in_specs = [head_spec, head_spec, head_spec,
                pl.BlockSpec((None, B_HEADS, 1), lambda bi, j, pt: (bi, 0, 0))]
    args = [q, k_new, v_new, lf_new]
    for cache in (cache_k, cache_v):
        for p in range(pps):
            in_specs.append(pl.BlockSpec((None, None, rows, B_HD),
                                         functools.partial(lambda bi, j, pt, p: (li, page_of(bi, j, pt, p), 0, 0), p=p)))
            args.append(cache)
    for p in range(pps):
        in_specs.append(pl.BlockSpec((None, None, B_HEADS, LANES),
                                     functools.partial(lambda bi, j, pt, p: (li, page_of(bi, j, pt, p), 0, 0), p=p)))
        args.append(cache_lf_t)
    grid_spec = pltpu.PrefetchScalarGridSpec(
        num_scalar_prefetch=1,
        grid=(n, steps),
        in_specs=in_specs,
        out_specs=pl.BlockSpec((None, B_HEADS, B_HD), lambda bi, j, pt: (bi, 0, 0)),
        scratch_shapes=[pltpu.VMEM((B_HEADS, 1), F32), pltpu.VMEM((B_HEADS, 1), F32),
                        pltpu.VMEM((B_HEADS, B_HD), F32), pltpu.VMEM((B_HEADS, 1), F32)],
    )
    return pl.pallas_call(
        functools.partial(_fox_sample_kernel, pps=pps, scale=FOX_SCALE),
        out_shape=jax.ShapeDtypeStruct((n, B_HEADS, B_HD), F32),
        grid_spec=grid_spec,
        compiler_params=_cparams("parallel", "arbitrary"),
        name="fox_sample",
    )(pt_flat, *args)


def _mla_sample_kernel(pt_ref, qn_ref, qr_ref, gain_ref, cnew_ref, rnew_ref, wukt_ref, wuv_ref, *refs, pps, scale):
    lat_refs = refs[:pps]
    rope_refs = refs[pps:2 * pps]
    o_ref, wq_ref, m_ref, l_ref, acc_ref = refs[2 * pps:]
    bi = pl.program_id(0)
    j = pl.program_id(1)
    nrow = C_HEADS * C_NOPE
    head_of_col = lax.broadcasted_iota(jnp.int32, (C_HEADS, nrow), 1) // C_NOPE
    own = head_of_col == lax.broadcasted_iota(jnp.int32, (C_HEADS, nrow), 0)

    @pl.when((bi == 0) & (j == 0))
    def _():
        wq_ref[0:nrow, :] = wukt_ref[...]

    def attend(cc, rr_t, valid):
        proj = lax.dot_general(wq_ref[...], cc, NT_DIMS, preferred_element_type=F32)
        kp = proj[0:nrow].reshape(C_HEADS, C_NOPE, cc.shape[0])
        rs = lax.rsqrt(jnp.sum(kp * kp, axis=1) * (1.0 / C_NOPE) + EPS)
        s = (proj[nrow:nrow + C_HEADS] * rs + _bdot(qr_ref[...], rr_t)) * scale
        if valid is not None:
            s = jnp.where(valid, s, NEG_INF)
        m_prev = m_ref[...]
        m_new = jnp.maximum(m_prev, jnp.max(s, axis=-1, keepdims=True))
        p = jnp.exp(s - m_new)
        alpha = jnp.exp(m_prev - m_new)
        l_ref[...] = alpha * l_ref[...] + jnp.sum(p, axis=-1, keepdims=True)
        acc_ref[...] = alpha * acc_ref[...] + _bdot(p, cc)
        m_ref[...] = m_new

    @pl.when(j == 0)
    def _():
        qg = jnp.concatenate([qn_ref[...]] * C_HEADS, axis=1) * gain_ref[...]
        qbd = jnp.where(own, qg, 0.0)
        wq_ref[nrow:nrow + C_HEADS, :] = _bdot(qbd, wukt_ref[...]).astype(BF16)
        m_ref[...] = jnp.full(m_ref.shape, NEG_INF, F32)
        l_ref[...] = jnp.zeros(l_ref.shape, F32)
        acc_ref[...] = jnp.zeros(acc_ref.shape, F32)
        first = lax.broadcasted_iota(jnp.int32, (1, LANES), 1) == 0
        attend(cnew_ref[...].astype(BF16), rnew_ref[...].astype(BF16), first)

    cc = jnp.concatenate([lat_refs[p][...] for p in range(pps)], axis=0).astype(BF16)
    rr_t = jnp.concatenate([rope_refs[p][...] for p in range(pps)], axis=1).astype(BF16)
    attend(cc, rr_t, None)

    @pl.when(j == pl.num_programs(1) - 1)
    def _():
        o_lat = acc_ref[...] / l_ref[...]
        full = _bdot(o_lat, wuv_ref[...])
        o_ref[...] = jnp.sum(jnp.where(own, full, 0.0), axis=0, keepdims=True)


def _mla_sample(q_nope, q_rope, k_gain, c_new, r_new, w_uk_t, w_uv, cache_lat, cache_rope_t, page_table, li):
    n, n_pages = page_table.shape
    page = cache_lat.shape[2]
    assert page == LANES
    pps = MLA_PPS
    assert n_pages % pps == 0
    steps = n_pages // pps
    pt_flat = page_table.reshape(-1)
    nrow = C_HEADS * C_NOPE
    gain_t = jnp.tile(k_gain.astype(F32), C_HEADS).reshape(1, nrow)

    def page_of(bi, j, pt, p):
        return pt[bi * n_pages + j * pps + p]

    in_specs = [
        pl.BlockSpec((None, C_HEADS, C_NOPE), lambda bi, j, pt: (bi, 0, 0)),
        pl.BlockSpec((None, C_HEADS, C_ROPE), lambda bi, j, pt: (bi, 0, 0)),
        pl.BlockSpec((1, nrow), lambda bi, j, pt: (0, 0)),
        pl.BlockSpec((None, page, C_KV_RANK), lambda bi, j, pt: (bi, 0, 0)),
        pl.BlockSpec((None, C_ROPE, page), lambda bi, j, pt: (bi, 0, 0)),
        pl.BlockSpec((nrow, C_KV_RANK), lambda bi, j, pt: (0, 0)),
        pl.BlockSpec((C_KV_RANK, C_HEADS * C_VD), lambda bi, j, pt: (0, 0)),
    ]
    args = [q_nope, q_rope, gain_t, c_new, r_new, w_uk_t, w_uv]
    for cache, blk in ((cache_lat, (page, C_KV_RANK)), (cache_rope_t, (C_ROPE, page))):
        for p in range(pps):
            in_specs.append(pl.BlockSpec((None, None) + blk,
                                         functools.partial(lambda bi, j, pt, p: (li, page_of(bi, j, pt, p), 0, 0), p=p)))
            args.append(cache)
    grid_spec = pltpu.PrefetchScalarGridSpec(
        num_scalar_prefetch=1,
        grid=(n, steps),
        in_specs=in_specs,
        out_specs=pl.BlockSpec((None, 1, C_HEADS * C_VD), lambda bi, j, pt: (bi, 0, 0)),
        scratch_shapes=[pltpu.VMEM((nrow + C_HEADS, C_KV_RANK), BF16),
                        pltpu.VMEM((C_HEADS, 1), F32), pltpu.VMEM((C_HEADS, 1), F32),
                        pltpu.VMEM((C_HEADS, C_KV_RANK), F32)],
    )
    return pl.pallas_call(
        functools.partial(_mla_sample_kernel, pps=pps, scale=MLA_SCALE),
        out_shape=jax.ShapeDtypeStruct((n, 1, C_HEADS * C_VD), F32),
        grid_spec=grid_spec,
        compiler_params=_cparams("arbitrary", "arbitrary"),
        name="mla_sample",
    )(pt_flat, *args)


AB_HALO = 8
_OFF_QB = A_CONV_CH + A_V
_OFF_KV = _OFF_QB + B_HEADS * B_HD
_OFF_SMALL = _OFF_KV + 2 * B_KV_HEADS * B_HD


def _softplus(x):
    return jnp.maximum(x, 0.0) + jnp.log(1.0 + jnp.exp(-jnp.abs(x)))


def _rms_rows(x, gain):
    return x * lax.rsqrt(jnp.mean(x * x, axis=-1, keepdims=True) + EPS) * gain


def _ab_pre_kernel(pos_ref, xc_ref, halo_ref, qb_in_ref, kv_in_ref, small_ref, cw_ref, add_ref, mul_ref, qn_ref, kn_ref,
                   qa_ref, ka_ref, va_ref, qb_ref, kb_ref, vb_ref, gt_ref, *, tm):
    pos = pos_ref[...]
    ext = jnp.concatenate([halo_ref[...], xc_ref[...]], axis=0)
    conv = ext[AB_HALO:AB_HALO + tm] * cw_ref[A_CONV - 1:A_CONV, :]
    for back in range(1, A_CONV):
        tap = ext[AB_HALO - back:AB_HALO - back + tm] * cw_ref[A_CONV - 1 - back:A_CONV - back, :]
        conv = conv + jnp.where(pos >= back, tap, 0.0)
    conv = conv * jax.nn.sigmoid(conv)
    for h in range(A_HEADS):
        sl = slice(h * A_DK, (h + 1) * A_DK)
        qh = conv[:, sl]
        kh = conv[:, A_QK + h * A_DK:A_QK + (h + 1) * A_DK]
        qa_ref[:, sl] = qh * (lax.rsqrt(jnp.sum(qh * qh, axis=-1, keepdims=True) + EPS) * (A_DK ** -0.5))
        ka_ref[:, sl] = kh * lax.rsqrt(jnp.sum(kh * kh, axis=-1, keepdims=True) + EPS)
    va_ref[...] = conv[:, 2 * A_QK:]
    for h in range(B_HEADS):
        sl = slice(h * B_HD, (h + 1) * B_HD)
        qb_ref[:, sl] = _rms_rows(qb_in_ref[:, sl], qn_ref[...])
    for h in range(B_KV_HEADS):
        sl = slice(h * B_HD, (h + 1) * B_HD)
        kb_ref[:, sl] = _rms_rows(kv_in_ref[:, sl], kn_ref[...])
    vb_ref[...] = kv_in_ref[:, B_KV_HEADS * B_HD:]
    u = small_ref[...] + add_ref[...]
    lane = lax.broadcasted_iota(jnp.int32, u.shape, 1)
    gt_ref[...] = jnp.where(lane < A_HEADS, mul_ref[...] * _softplus(u),
                            jnp.where(lane < 2 * A_HEADS, jax.nn.sigmoid(u),
                                      jnp.where(lane < 2 * A_HEADS + B_HEADS, -_softplus(-u), 0.0)))


def _ab_pre_prompt(proj, pos, conv_w, a_log, dt_bias, fgate_bias, q_norm, k_norm):
    m = proj.shape[0]
    tm = _pick_tile(m, (256, 128))
    kvw = 2 * B_KV_HEADS * B_HD
    pad = lambda v: jnp.pad(v, (0, LANES - v.shape[0])).reshape(1, LANES)
    add_row = pad(jnp.concatenate([dt_bias.astype(F32), jnp.zeros((A_HEADS,), F32), fgate_bias.astype(F32)]))
    mul_row = pad(-jnp.exp(a_log.astype(F32)))
    row = lambda i: (i, 0)
    fixed = lambda i: (0, 0)
    return pl.pallas_call(
        functools.partial(_ab_pre_kernel, tm=tm),
        out_shape=(jax.ShapeDtypeStruct((m, A_QK), F32), jax.ShapeDtypeStruct((m, A_QK), F32),
                   jax.ShapeDtypeStruct((m, A_V), F32), jax.ShapeDtypeStruct((m, B_HEADS * B_HD), F32),
                   jax.ShapeDtypeStruct((m, B_KV_HEADS * B_HD), F32), jax.ShapeDtypeStruct((m, B_KV_HEADS * B_HD), F32),
                   jax.ShapeDtypeStruct((m, LANES), F32)),
        grid=(m // tm,),
        in_specs=[pl.BlockSpec((tm, 1), row),
                  pl.BlockSpec((tm, A_CONV_CH), row),
                  pl.BlockSpec((AB_HALO, A_CONV_CH), lambda i: (jnp.maximum(i * (tm // AB_HALO) - 1, 0), 0)),
                  pl.BlockSpec((tm, B_HEADS * B_HD), lambda i: (i, _OFF_QB // (B_HEADS * B_HD))),
                  pl.BlockSpec((tm, kvw), lambda i: (i, _OFF_KV // kvw)),
                  pl.BlockSpec((tm, LANES), lambda i: (i, _OFF_SMALL // LANES)),
                  pl.BlockSpec((A_CONV, A_CONV_CH), fixed),
                  pl.BlockSpec((1, LANES), fixed), pl.BlockSpec((1, LANES), fixed),
                  pl.BlockSpec((1, B_HD), fixed), pl.BlockSpec((1, B_HD), fixed)],
        out_specs=(pl.BlockSpec((tm, A_QK), row), pl.BlockSpec((tm, A_QK), row), pl.BlockSpec((tm, A_V), row),
                   pl.BlockSpec((tm, B_HEADS * B_HD), row), pl.BlockSpec((tm, B_KV_HEADS * B_HD), row),
                   pl.BlockSpec((tm, B_KV_HEADS * B_HD), row), pl.BlockSpec((tm, LANES), row)),
        compiler_params=_cparams("parallel"),
        name="ab_pre",
    )(pos, proj, proj, proj, proj, proj, conv_w.astype(F32), add_row, mul_row,
      q_norm.astype(F32).reshape(1, B_HD), k_norm.astype(F32).reshape(1, B_HD))


C_QN = C_HEADS * C_NOPE


def _rotate(x, cos, sin):
    half = C_ROPE // 2
    x1, x2 = x[:, :half], x[:, half:]
    return jnp.concatenate([x1 * cos - x2 * sin, x2 * cos + x1 * sin], axis=1)


def _mla_pre_kernel(q_ref, lat_ref, cs_ref, gqn_ref, gqr_ref, gc_ref, gkr_ref, qn_ref, qr_ref, c_ref, kr_ref):
    cos = cs_ref[:, :C_ROPE // 2]
    sin = cs_ref[:, C_ROPE // 2:C_ROPE]
    for h in range(C_HEADS):
        sl = slice(h * C_NOPE, (h + 1) * C_NOPE)
        qn_ref[:, sl] = _rms_rows(q_ref[:, sl], gqn_ref[...])
        rl = slice(h * C_ROPE, (h + 1) * C_ROPE)
        qr_ref[:, rl] = _rotate(_rms_rows(q_ref[:, C_QN + h * C_ROPE:C_QN + (h + 1) * C_ROPE], gqr_ref[...]), cos, sin)
    c_ref[...] = _rms_rows(lat_ref[:, C_Q_RANK:C_Q_RANK + C_KV_RANK], gc_ref[...])
    kr = _rotate(_rms_rows(lat_ref[:, C_Q_RANK + C_KV_RANK:C_Q_RANK + C_KV_RANK + C_ROPE], gkr_ref[...]), cos, sin)
    kr_ref[...] = jnp.concatenate([kr, jnp.zeros((kr.shape[0], LANES - C_ROPE), F32)], axis=1)


def _mla_pre(q, lat, cos_sin, g_q_nope, g_q_rope, g_c, g_k_rope):
    m = q.shape[0]
    tm = _pick_tile(m, (256, 128))
    row = lambda i: (i, 0)
    fixed = lambda i: (0, 0)
    g2 = lambda g: g.astype(F32).reshape(1, -1)
    return pl.pallas_call(
        _mla_pre_kernel,
        out_shape=(jax.ShapeDtypeStruct((m, C_QN), F32), jax.ShapeDtypeStruct((m, C_HEADS * C_ROPE), F32),
                   jax.ShapeDtypeStruct((m, C_KV_RANK), F32), jax.ShapeDtypeStruct((m, LANES), F32)),
        grid=(m // tm,),
        in_specs=[pl.BlockSpec((tm, q.shape[1]), row), pl.BlockSpec((tm, lat.shape[1]), row),
                  pl.BlockSpec((tm, C_ROPE), row),
                  pl.BlockSpec((1, C_NOPE), fixed), pl.BlockSpec((1, C_ROPE), fixed),
                  pl.BlockSpec((1, C_KV_RANK), fixed), pl.BlockSpec((1, C_ROPE), fixed)],
        out_specs=(pl.BlockSpec((tm, C_QN), row), pl.BlockSpec((tm, C_HEADS * C_ROPE), row),
                   pl.BlockSpec((tm, C_KV_RANK), row), pl.BlockSpec((tm, LANES), row)),
        compiler_params=_cparams("parallel"),
        name="mla_pre",
    )(q, lat, cos_sin, g2(g_q_nope), g2(g_q_rope), g2(g_c), g2(g_k_rope))


def _head_rms_kernel(x_ref, g_ref, o_ref, *, heads, hd):
    for h in range(heads):
        sl = slice(h * hd, (h + 1) * hd)
        o_ref[:, sl] = _rms_rows(x_ref[:, sl], g_ref[...])


def _head_rms(x, gain, heads, hd):
    m = x.shape[0]
    tm = _pick_tile(m, (256, 128))
    w = heads * hd
    return pl.pallas_call(
        functools.partial(_head_rms_kernel, heads=heads, hd=hd),
        out_shape=jax.ShapeDtypeStruct((m, w), F32),
        grid=(m // tm,),
        in_specs=[pl.BlockSpec((tm, w), lambda i: (i, 0)), pl.BlockSpec((1, hd), lambda i: (0, 0))],
        out_specs=pl.BlockSpec((tm, w), lambda i: (i, 0)),
        compiler_params=_cparams("parallel"),
        name="head_rms",
    )(x, gain.astype(F32).reshape(1, hd))


def _router_kernel(x_ref, g_ref, w_ref, b_ref, h_ref, eid_ref, gate_ref):
    x = x_ref[...]
    xn = x * lax.rsqrt(jnp.mean(x * x, axis=-1, keepdims=True) + EPS) * g_ref[...]
    h_ref[...] = xn
    lg = _hdot(xn, w_ref[...]) + b_ref[...]
    lane = lax.broadcasted_iota(jnp.int32, lg.shape, 1)
    lane_f = lane.astype(F32)
    ninf = -jnp.inf
    is_g = lane < N_GROUPS
    gmax = jnp.max(jnp.where(is_g, lg, ninf), axis=-1, keepdims=True)
    gi = jnp.min(jnp.where(is_g & (lg == gmax), lane_f, float(LANES)), axis=-1, keepdims=True)
    gsum = jnp.sum(jnp.exp(jnp.where(is_g, lg - gmax, ninf)), axis=-1, keepdims=True)
    grp_p = 1.0 / gsum
    e_lane = lane - N_GROUPS
    group_of = lax.shift_right_arithmetic(e_lane, int(math.log2(EXPERTS_PER_GROUP)))
    sel = (e_lane >= 0) & (e_lane < N_EXPERTS) & (group_of.astype(F32) == gi)
    m1 = jnp.max(jnp.where(sel, lg, ninf), axis=-1, keepdims=True)
    i1 = jnp.min(jnp.where(sel & (lg == m1), lane_f, 2.0 * LANES), axis=-1, keepdims=True)
    sel2 = sel & (lane_f != i1)
    m2 = jnp.max(jnp.where(sel2, lg, ninf), axis=-1, keepdims=True)
    i2 = jnp.min(jnp.where(sel2 & (lg == m2), lane_f, 2.0 * LANES), axis=-1, keepdims=True)
    r = jnp.exp(m2 - m1)
    g1 = grp_p / (1.0 + r)
    g2 = g1 * r
    eid_ref[...] = jnp.where(lane == 0, i1, jnp.where(lane == 1, i2, float(N_GROUPS))).astype(jnp.int32) - N_GROUPS
    gate_ref[...] = jnp.where(lane == 0, g1, jnp.where(lane == 1, g2, 0.0))


def _router(x, gain, w_router, b_router):
    m, d = x.shape
    tm = _pick_tile(m, (256, 128))
    row = lambda i: (i, 0)
    fixed = lambda i: (0, 0)
    return pl.pallas_call(
        _router_kernel,
        out_shape=(jax.ShapeDtypeStruct((m, d), F32), jax.ShapeDtypeStruct((m, LANES), jnp.int32),
                   jax.ShapeDtypeStruct((m, LANES), F32)),
        grid=(m // tm,),
        in_specs=[pl.BlockSpec((tm, d), row), pl.BlockSpec((1, d), fixed), pl.BlockSpec((d, LANES), fixed),
                  pl.BlockSpec((1, LANES), fixed)],
        out_specs=(pl.BlockSpec((tm, d), row), pl.BlockSpec((tm, LANES), row), pl.BlockSpec((tm, LANES), row)),
        compiler_params=_cparams("parallel"),
        name="moe_router",
    )(x, gain.reshape(1, d).astype(F32), w_router, b_router)


def _expert_kernel(be_ref, na_ref, src_ref, nsrc_ref, dst_ref, h_hbm, wg_ref, wu_ref, wd_ref, out_hbm,
                   xbuf, ybuf, gsem, ssem):
    i = pl.program_id(0)
    na = na_ref[0]
    slot = i % 2

    def gather_copies(idx_ref, s):
        return [pltpu.make_async_copy(h_hbm.at[pl.ds(idx_ref[0, r], 1)], xbuf.at[s, pl.ds(r, 1)], gsem.at[s])
                for r in range(MOE_BLOCK)]

    def scatter_copies(s, use_dst):
        return [pltpu.make_async_copy(ybuf.at[s, pl.ds(r, 1)],
                                      out_hbm.at[pl.ds(dst_ref[0, r] if use_dst else 0, 1)], ssem.at[s])
                for r in range(MOE_BLOCK)]

    @pl.when(i < na)
    def _():
        @pl.when(i == 0)
        def _():
            for c in gather_copies(src_ref, slot):
                c.start()

        for c in gather_copies(src_ref, slot):
            c.wait()

        @pl.when(i + 1 < na)
        def _():
            for c in gather_copies(nsrc_ref, 1 - slot):
                c.start()

        @pl.when(i >= 2)
        def _():
            for c in scatter_copies(slot, False):
                c.wait()

        x = xbuf[slot].astype(BF16)
        gate = jnp.dot(x, wg_ref[...].astype(BF16), preferred_element_type=F32)
        up = jnp.dot(x, wu_ref[...].astype(BF16), preferred_element_type=F32)
        hidden = gate * jax.nn.sigmoid(gate) * up
        ybuf[slot] = jnp.dot(hidden.astype(BF16), wd_ref[...].astype(BF16), preferred_element_type=F32)
        for c in scatter_copies(slot, True):
            c.start()

        @pl.when(i == na - 1)
        def _():
            for c in scatter_copies(slot, False):
                c.wait()

            @pl.when(i >= 1)
            def _():
                for c in scatter_copies(1 - slot, False):
                    c.wait()


def _experts(h, plan, w_gate, w_up, w_down, layer):
    block_expert, n_active, src_row, dst_row = plan
    n_blocks = block_expert.shape[0]
    n_slots = n_blocks * MOE_BLOCK
    d = h.shape[1]
    de = w_gate.shape[-1]
    src3 = src_row.reshape(n_blocks, 1, MOE_BLOCK)
    dst3 = dst_row.reshape(n_blocks, 1, MOE_BLOCK)
    smem_blk = lambda f: pl.BlockSpec((None, 1, MOE_BLOCK), f, memory_space=pltpu.SMEM)
    grid_spec = pltpu.PrefetchScalarGridSpec(
        num_scalar_prefetch=2,
        grid=(n_blocks,),
        in_specs=[smem_blk(lambda i, be, na: (i, 0, 0)),
                  smem_blk(lambda i, be, na: (jnp.minimum(i + 1, n_blocks - 1), 0, 0)),
                  smem_blk(lambda i, be, na: (i, 0, 0)),
                  pl.BlockSpec(memory_space=pl.ANY),
                  pl.BlockSpec((None, None, d, de), lambda i, be, na: (layer, be[i], 0, 0)),
                  pl.BlockSpec((None, None, d, de), lambda i, be, na: (layer, be[i], 0, 0)),
                  pl.BlockSpec((None, None, de, d), lambda i, be, na: (layer, be[i], 0, 0))],
        out_specs=pl.BlockSpec(memory_space=pl.ANY),
        scratch_shapes=[pltpu.VMEM((2, MOE_BLOCK, d), F32), pltpu.VMEM((2, MOE_BLOCK, d), F32),
                        pltpu.SemaphoreType.DMA((2,)), pltpu.SemaphoreType.DMA((2,))],
    )
    return pl.pallas_call(
        _expert_kernel,
        out_shape=jax.ShapeDtypeStruct((h.shape[0] * TOP_K + n_slots, d), F32),
        grid_spec=grid_spec,
        compiler_params=_cparams("arbitrary"),
        name="moe_experts",
    )(block_expert, n_active, src3, src3, dst3, h, w_gate, w_up, w_down)


def _combine_kernel(x_ref, y0_ref, y1_ref, g_ref, o_ref):
    g = g_ref[...]
    o_ref[...] = x_ref[...] + g[:, 0:1] * y0_ref[...] + g[:, 1:2] * y1_ref[...]


def _combine(x, y_rows, gate):
    m, d = x.shape
    tm = _pick_tile(m, (256, 128))
    row = lambda i: (i, 0)
    return pl.pallas_call(
        _combine_kernel,
        out_shape=jax.ShapeDtypeStruct((m, d), F32),
        grid=(m // tm,),
        in_specs=[pl.BlockSpec((tm, d), row), pl.BlockSpec((tm, d), row),
                  pl.BlockSpec((tm, d), lambda i: (i + m // tm, 0)), pl.BlockSpec((tm, LANES), row)],
        out_specs=pl.BlockSpec((tm, d), row),
        compiler_params=_cparams("parallel"),
        name="moe_combine",
    )(x, y_rows, y_rows, gate)


def _moe_plan(eid):
    n = eid.shape[0]
    nk = n * TOP_K
    flat_e = eid.reshape(-1)
    order = jnp.argsort(flat_e).astype(jnp.int32)
    counts = jnp.sum(flat_e[:, None] == jnp.arange(N_EXPERTS, dtype=jnp.int32)[None, :], axis=0, dtype=jnp.int32)
    padded = (counts + MOE_BLOCK - 1) // MOE_BLOCK * MOE_BLOCK
    pad_end = jnp.cumsum(padded)
    pad_start = pad_end - padded
    seg_start = jnp.cumsum(counts) - counts
    n_slots = _round_up(nk + N_EXPERTS * (MOE_BLOCK - 1), MOE_BLOCK)
    n_blocks = n_slots // MOE_BLOCK
    n_active = (pad_end[-1] // MOE_BLOCK).astype(jnp.int32)
    blk = jnp.arange(n_blocks, dtype=jnp.int32)
    block_expert = jnp.minimum(jnp.searchsorted(pad_end, blk * MOE_BLOCK, side='right'), N_EXPERTS - 1).astype(jnp.int32)
    block_expert = jnp.where(blk < n_active, block_expert, block_expert[n_active - 1])
    slot = jnp.arange(n_slots, dtype=jnp.int32)
    e_s = jnp.repeat(block_expert, MOE_BLOCK)
    pos = slot - pad_start[e_s]
    valid = (pos < counts[e_s]) & (slot < pad_end[-1])
    pair = order[jnp.clip(seg_start[e_s] + pos, 0, nk - 1)]
    src_row = jnp.where(valid, pair // TOP_K, 0).astype(jnp.int32)
    dst_row = jnp.where(valid, (pair % TOP_K) * n + pair // TOP_K, nk + slot).astype(jnp.int32)
    return block_expert, n_active.reshape(1), src_row, dst_row


def _hier_moe(x, layer, gain, w_group, b_group, w_expert, b_expert, w_gate, w_up, w_down):
    m, d = x.shape
    w_router = jnp.zeros((d, LANES), F32).at[:, :N_GROUPS].set(w_group).at[:, N_GROUPS:N_GROUPS + N_EXPERTS].set(w_expert)
    b_router = jnp.zeros((1, LANES), F32).at[0, :N_GROUPS].set(b_group).at[0, N_GROUPS:N_GROUPS + N_EXPERTS].set(b_expert)
    h, eid, gate = _router(x, gain, w_router, b_router)
    plan = _moe_plan(eid[:, :TOP_K])
    y_pairs = _experts(h, plan, w_gate, w_up, w_down, layer)
    return _combine(x, y_pairs, gate)


def _rms(x, g):
    xf = x.astype(F32)
    return xf * lax.rsqrt(jnp.mean(xf * xf, axis=-1, keepdims=True) + EPS) * g.astype(F32)


def _l2(x):
    return x * lax.rsqrt(jnp.sum(x * x, axis=-1, keepdims=True) + EPS)


def _pack_ab_w_in(w):
    sizes = (A_CONV_CH, A_V, A_HEADS, A_HEADS, B_HEADS * B_HD, B_KV_HEADS * B_HD, B_KV_HEADS * B_HD, B_HEADS)
    qkv, z, a_raw, b_raw, q_b, k_b, v_b, f_raw = jnp.split(w, np.cumsum(sizes)[:-1].tolist(), axis=-1)
    small = jnp.concatenate([a_raw, b_raw, f_raw], axis=-1)
    small = jnp.pad(small, ((0, 0), (0, LANES - small.shape[-1])))
    packed = jnp.concatenate([qkv, z, q_b, k_b, v_b, small], axis=-1)
    n_pad = _round_up(packed.shape[-1], 768)
    return jnp.pad(packed, ((0, 0), (0, n_pad - packed.shape[-1]))).astype(BF16)


def _ab_pre(proj, conv_buf, conv_w, a_log, dt_bias, fgate_bias, q_norm, k_norm):
    b, t, _ = proj.shape
    o = 0
    qkv_a = proj[..., o:o + A_CONV_CH]; o += A_CONV_CH
    z = proj[..., o:o + A_V]; o += A_V
    q_b = proj[..., o:o + B_HEADS * B_HD]; o += B_HEADS * B_HD
    k_b = proj[..., o:o + B_KV_HEADS * B_HD]; o += B_KV_HEADS * B_HD
    v_b = proj[..., o:o + B_KV_HEADS * B_HD]; o += B_KV_HEADS * B_HD
    a_raw = proj[..., o:o + A_HEADS]
    b_raw = proj[..., o + A_HEADS:o + 2 * A_HEADS]
    f_raw = proj[..., o + 2 * A_HEADS:o + 2 * A_HEADS + B_HEADS]
    xp = jnp.concatenate([conv_buf.astype(F32), qkv_a], axis=1)
    conv = xp[:, 0:t] * conv_w[0]
    for i in range(1, A_CONV):
        conv = conv + xp[:, i:i + t] * conv_w[i]
    conv = jax.nn.silu(conv)
    new_buf = xp[:, t:]
    qa = _l2(conv[..., :A_QK].reshape(b, t, A_HEADS, A_DK)) * (A_DK ** -0.5)
    ka = _l2(conv[..., A_QK:2 * A_QK].reshape(b, t, A_HEADS, A_DK))
    va = conv[..., 2 * A_QK:].reshape(b, t, A_HEADS, A_DV)
    beta = jax.nn.sigmoid(b_raw)
    g = -jnp.exp(a_log.astype(F32)) * jax.nn.softplus(a_raw + dt_bias.astype(F32))
    qb = _rms(q_b.reshape(b, t, B_HEADS, B_HD), q_norm)
    kb = _rms(k_b.reshape(b, t, B_KV_HEADS, B_HD), k_norm)
    vb = v_b.reshape(b, t, B_KV_HEADS, B_HD)
    logf = jax.nn.log_sigmoid(f_raw + fgate_bias.astype(F32))
    return (qa, ka, va, g, beta, z, new_buf), (qb, kb, vb, logf)


def _gate_lanes(g, beta):
    both = jnp.concatenate([g, beta], axis=-1)
    return jnp.pad(both, [(0, 0)] * (both.ndim - 1) + [(0, LANES - both.shape[-1])])


def _pad_seq(a, tp):
    return jnp.pad(a, [(0, 0), (0, tp - a.shape[1])] + [(0, 0)] * (a.ndim - 2))


def _heads_first(a, tp):
    return jnp.swapaxes(_pad_seq(a, tp), 1, 2).astype(BF16)


def _gdn_gated_out(o_gdn, z, gdn_norm):
    b, t = o_gdn.shape[:2]
    og = _rms(o_gdn, gdn_norm) * jax.nn.silu(z.reshape(b, t, A_HEADS, A_DV))
    return og.reshape(b, t, A_V)


def kernel(x_prompt, x_sample, state_gdn, state_gdn_conv, cache_fox_k, cache_fox_v, cache_fox_logf, cache_mla_latent, cache_mla_rope, page_table, meta_tokens, norm_mix, norm_ffn, ab_w_in, ab_conv_w, ab_a_log, ab_dt_bias, ab_gdn_norm, ab_fgate_bias, ab_q_norm, ab_k_norm, ab_w_out, c_w_in, c_q_a_norm, c_kv_a_norm, c_w_q_b, c_w_uk, c_w_uv, c_q_nope_norm, c_q_rope_norm, c_k_nope_norm, c_k_rope_norm, c_w_out, moe_w_group, moe_b_group, moe_w_expert, moe_b_expert, moe_w_gate, moe_w_up, moe_w_down):
    b, seq, d = x_prompt.shape
    ns, ds, _ = x_sample.shape
    assert ds == 1
    depth = norm_mix.shape[0]
    t = seq + N_META
    n_p = b * t
    n_tok = n_p + ns
    m_rows = _round_up(n_tok, 256)
    past = page_table.shape[1] * cache_fox_k.shape[2]
    tp_attn = _round_up(t, 256)
    tp_gdn = _round_up(t, A_CHUNK)

    meta = jnp.broadcast_to(meta_tokens.astype(F32)[None], (b, N_META, d))
    xp0 = jnp.concatenate([meta, x_prompt], axis=1).reshape(n_p, d)
    x = jnp.concatenate([xp0, x_sample.reshape(ns, d), jnp.zeros((m_rows - n_tok, d), F32)], axis=0)

    n_layers_ab, pool = cache_fox_k.shape[:2]
    fox_k_rows = cache_fox_k.reshape(n_layers_ab, pool, -1, B_HD)
    fox_v_rows = cache_fox_v.reshape(n_layers_ab, pool, -1, B_HD)
    fox_lf_t = jnp.swapaxes(cache_fox_logf, 2, 3)
    mla_rope_t = jnp.swapaxes(cache_mla_rope, 2, 3)
    rows = jnp.arange(m_rows, dtype=jnp.int32)
    pos_rows = jnp.where(rows < n_p, rows % t, past)
    inv_freq = ROPE_THETA ** (-jnp.arange(C_ROPE // 2, dtype=F32) / (C_ROPE // 2))
    angle = pos_rows.astype(F32)[:, None] * inv_freq[None, :]
    cos_sin = jnp.concatenate([jnp.cos(angle), jnp.sin(angle)], axis=1)
    prompt_rows = lambda a: a[:n_p].reshape(b, t, -1)
    outs = {name: [] for name in ("pS", "pconv", "pk", "pv", "pf", "pc", "pr", "sS", "sconv", "sk", "sv", "sf", "sc", "sr")}

    for layer in range(depth):
        li = layer // 2
        if layer % 2 == 0:
            proj = _matmul(x, _pack_ab_w_in(ab_w_in[li]), gain=norm_mix[layer], name="ab_in_proj")
            wts = (ab_conv_w[li], ab_a_log[li], ab_dt_bias[li], ab_fgate_bias[li], ab_q_norm[li], ab_k_norm[li])
            qa, ka, va, qb, kb, vb, gates = _ab_pre_prompt(proj, pos_rows.reshape(m_rows, 1), *wts)
            gates = prompt_rows(gates)
            lf = gates[..., 2 * A_HEADS:2 * A_HEADS + B_HEADS]
            qb = prompt_rows(qb).reshape(b, t, B_HEADS, B_HD)
            kb = prompt_rows(kb).reshape(b, t, B_KV_HEADS, B_HD)
            vb = prompt_rows(vb).reshape(b, t, B_KV_HEADS, B_HD)
            z = proj[:n_p, A_CONV_CH:A_CONV_CH + A_V].reshape(b, t, A_V)
            buf = proj[:n_p, :A_CONV_CH].reshape(b, t, A_CONV_CH)[:, t - (A_CONV - 1):]
            o_a, s_new = _gdn_chunked(_pad_seq(prompt_rows(qa), tp_gdn), _pad_seq(prompt_rows(ka), tp_gdn),
                                      _pad_seq(prompt_rows(va), tp_gdn), _pad_seq(gates, tp_gdn))
            o_a = o_a[:, :t].reshape(b, t, A_HEADS, A_DV)
            f_cum = jnp.cumsum(lf, axis=1)
            kbias = jnp.swapaxes(_pad_seq(-f_cum, tp_attn), 1, 2)[:, :, None, :]
            o_b = _flash_attention(_heads_first(qb, tp_attn), _heads_first(kb, tp_attn), _heads_first(vb, tp_attn),
                                   kbias, FOX_SCALE, hb=B_GROUP, tq=256, tk=256, name="fox_prompt")
            o_b = jnp.swapaxes(o_b, 1, 2)[:, :t].reshape(b, t, B_HEADS * B_HD)
            mixed_p = jnp.concatenate([_gdn_gated_out(o_a, z, ab_gdn_norm[li]), o_b], axis=-1)
            outs["pS"].append(s_new); outs["pconv"].append(buf)
            outs["pk"].append(kb); outs["pv"].append(vb); outs["pf"].append(lf)
            (qa, ka, va, g, beta, z, buf), (qb, kb, vb, lf) = _ab_pre(
                proj[n_p:n_tok].reshape(ns, 1, -1), state_gdn_conv[li], *wts)
            o_a, s_new = _gdn_step(state_gdn[li], qa[:, 0], ka[:, 0], va[:, 0], _gate_lanes(g[:, 0], beta[:, 0]))
            rep = lambda a: jnp.repeat(a[:, 0], B_GROUP, axis=1)
            o_b = _fox_sample(qb[:, 0], rep(kb), rep(vb), lf[:, 0, :, None], fox_k_rows, fox_v_rows, fox_lf_t,
                              page_table, li)
            mixed_s = jnp.concatenate([_gdn_gated_out(o_a[:, None], z, ab_gdn_norm[li]),
                                       o_b.reshape(ns, 1, B_HEADS * B_HD)], axis=-1)
            outs["sS"].append(s_new); outs["sconv"].append(buf)
            outs["sk"].append(kb); outs["sv"].append(vb); outs["sf"].append(lf)
            mixed = jnp.concatenate([mixed_p.reshape(n_p, -1), mixed_s.reshape(ns, -1),
                                     jnp.zeros((m_rows - n_tok, mixed_p.shape[-1]), F32)], axis=0)
            x = _matmul(mixed, ab_w_out[li].astype(BF16), residual=x, name="ab_out_proj")
        else:
            c_in = c_w_in[li]
            n_in = _round_up(c_in.shape[1], LANES)
            lat = _matmul(x, jnp.pad(c_in, ((0, 0), (0, n_in - c_in.shape[1]))).astype(BF16),
                          gain=norm_mix[layer], name="mla_in_proj")
            w_q = c_w_q_b[li]
            w_q = jnp.concatenate([w_q[..., :C_NOPE].reshape(C_Q_RANK, -1), w_q[..., C_NOPE:].reshape(C_Q_RANK, -1)], axis=1)
            q = _matmul(lat[:, :C_Q_RANK], w_q.astype(BF16), gain=c_q_a_norm[li], name="mla_q_proj")
            q_nope, q_rope, c, kr = _mla_pre(q, lat, cos_sin, c_q_nope_norm[li], c_q_rope_norm[li],
                                             c_kv_a_norm[li], c_k_rope_norm[li])
            kr = kr[:, :C_ROPE]
            w_uk = c_w_uk[li].reshape(C_KV_RANK, -1)
            w_uv = c_w_uv[li].reshape(C_KV_RANK, -1)
            qn_p = prompt_rows(q_nope).reshape(b, t, C_HEADS, C_NOPE)
            qr_p = prompt_rows(q_rope).reshape(b, t, C_HEADS, C_ROPE)
            c_p = prompt_rows(c)
            kr_p = prompt_rows(kr)
            kv = _matmul(c, jnp.concatenate([w_uk, w_uv], axis=1).astype(BF16), name="mla_kv_proj")
            k_nope = prompt_rows(_head_rms(kv, c_k_nope_norm[li], C_HEADS, C_NOPE)).reshape(b, t, C_HEADS, C_NOPE)
            v_p = kv[:n_p, C_HEADS * C_NOPE:].reshape(b, t, C_HEADS, C_VD)
            zpad = jnp.zeros((b, t, C_HEADS, C_NOPE - C_ROPE), F32)
            q_full = jnp.concatenate([qn_p, qr_p, zpad], axis=-1)
            k_full = jnp.concatenate([k_nope, jnp.broadcast_to(kr_p[:, :, None, :], (b, t, C_HEADS, C_ROPE)), zpad], axis=-1)
            o_p = _flash_attention(_heads_first(q_full, tp_attn), _heads_first(k_full, tp_attn), _heads_first(v_p, tp_attn),
                                   None, MLA_SCALE, hb=4, tq=256, tk=256, name="mla_prompt")
            o_p = jnp.swapaxes(o_p, 1, 2)[:, :t].reshape(n_p, C_HEADS * C_VD)
            outs["pc"].append(c_p); outs["pr"].append(kr_p)
            qn_s = q_nope[n_p:n_tok].reshape(ns, C_HEADS, C_NOPE)
            qr_s = q_rope[n_p:n_tok].reshape(ns, C_HEADS, C_ROPE)
            c_s = c[n_p:n_tok]
            kr_s = kr[n_p:n_tok]
            c_first = jnp.pad(c_s[:, None, :], ((0, 0), (0, LANES - 1), (0, 0)))
            r_first = jnp.pad(kr_s[:, :, None], ((0, 0), (0, 0), (0, LANES - 1)))
            o_s = _mla_sample(qn_s, qr_s.astype(BF16), c_k_nope_norm[li], c_first, r_first,
                              w_uk.T.astype(BF16), w_uv.astype(BF16), cache_mla_latent, mla_rope_t, page_table, li)
            outs["sc"].append(c_s.reshape(ns, 1, C_KV_RANK)); outs["sr"].append(kr_s.reshape(ns, 1, C_ROPE))
            o_all = jnp.concatenate([o_p, o_s.reshape(ns, -1), jnp.zeros((m_rows - n_tok, o_p.shape[-1]), F32)], axis=0)
            x = _matmul(o_all, c_w_out[li].astype(BF16), residual=x, name="mla_out_proj")
        x = _hier_moe(x, layer, norm_ffn[layer], moe_w_group[layer], moe_b_group[layer], moe_w_expert[layer],
                      moe_b_expert[layer], moe_w_gate, moe_w_up, moe_w_down)

    y_prompt = x[:n_p].reshape(b, t, d)[:, N_META:]
    y_sample = x[n_p:n_tok].reshape(ns, ds, d)
    st = lambda name: jnp.stack(outs[name])
    return (y_prompt, y_sample, st("pS"), st("pconv"), st("pk"), st("pv"), st("pf"), st("pc"), st("pr"),
            st("sS"), st("sconv"), st("sk"), st("sv"), st("sf"), st("sc"), st("sr"))
```
